```python
import jax
import jax.numpy as jnp
from jax import lax
import numpy as np

D_MODEL = 1024
BATCH = 8
SEQ = 4096
DEPTH = 4

GRID_W = 64
CTX_LEN = 256

N_MIXERS = 3
MIX_POOL, MIX_MLA, MIX_NA = 0, 1, 2
N_POOL_LAYERS = (DEPTH + 2) // 3
N_MLA_LAYERS = (DEPTH + 1) // 3
N_NA_LAYERS = DEPTH // 3

RMS_EPS = 1e-6

POOL_WINDOWS = (2, 4, 8, 16)
POOL_GROUPS = len(POOL_WINDOWS)
POOL_GROUP_DIM = D_MODEL // POOL_GROUPS

MLA_HEADS = D_MODEL // 128
MLA_NOPE = 128
MLA_ROPE = 64
MLA_V = 128
MLA_Q_LORA = 384
MLA_KV_LORA = 256
MLA_IN = MLA_Q_LORA + MLA_KV_LORA + MLA_ROPE
MLA_SCALE = (MLA_NOPE + MLA_ROPE) ** -0.5
ROPE_BASE = 10000.0
Q_BLOCK = 128

NA_HEADS = D_MODEL // 64
NA_HEAD_DIM = 64
NA_WIDTH = NA_HEADS * NA_HEAD_DIM
NA_ROWS = 8
NA_COLS = 16
NA_SCALE = NA_HEAD_DIM ** -0.5

FFN_HIDDEN = -(-8 * D_MODEL // (3 * 256)) * 256

kernel_name = "hybrid_pool_mla_natten_prefix_dit"


def rmsnorm(x, g):
    xf = x.astype(jnp.float32)
    y = xf * lax.rsqrt(jnp.mean(xf * xf, axis=-1, keepdims=True) + RMS_EPS)
    return (y * g.astype(jnp.float32)).astype(x.dtype)


def modulate(h, shift, scale):
    return h * (1 + scale) + shift


def adaln(cond, w, b):
    m = jax.nn.silu(cond) @ w + b
    return jnp.split(m, 6, axis=-1)


def swiglu(h, w_gu, w_down):
    g, u = jnp.split(h @ w_gu, 2, axis=-1)
    return (jax.nn.silu(g) * u) @ w_down


def attend(q, k, v, scale):
    s = jnp.einsum('bqhd,bkhd->bhqk', q, k).astype(jnp.float32) * scale
    p = jax.nn.softmax(s, axis=-1).astype(v.dtype)
    return jnp.einsum('bhqk,bkhd->bqhd', p, v)


def blocked_attend(q, k, v, scale):
    B, L, H, dk = q.shape
    nb = L // Q_BLOCK
    qb = q.reshape(B, nb, Q_BLOCK, H, dk).transpose(1, 0, 2, 3, 4)
    out = lax.map(lambda qi: attend(qi, k, v, scale), qb)
    return out.transpose(1, 0, 2, 3, 4).reshape(B, L, H, v.shape[-1])


def axial_rope_tables(L):
    t = jnp.arange(L)
    row = (t // GRID_W).astype(jnp.float32)
    col = (t % GRID_W).astype(jnp.float32)
    half = MLA_ROPE // 2
    inv = ROPE_BASE ** (-jnp.arange(0, half, 2, dtype=jnp.float32) / half)
    ang = jnp.concatenate([row[:, None] * inv, col[:, None] * inv], axis=-1)
    return jnp.cos(ang), jnp.sin(ang)


def apply_rope(x, cos, sin):
    half = x.shape[-1] // 2
    x1 = x[..., :half].astype(jnp.float32)
    x2 = x[..., half:].astype(jnp.float32)
    return jnp.concatenate([x1 * cos - x2 * sin, x1 * sin + x2 * cos], axis=-1).astype(x.dtype)


def pool_mixer(h, w, scale):
    B, L, D = h.shape
    hf = h.astype(jnp.float32)
    cs = jnp.concatenate([jnp.zeros((B, 1, D), jnp.float32), jnp.cumsum(hf, axis=1)], axis=1)
    pos = jnp.arange(L)
    groups = []
    for g, win in enumerate(POOL_WINDOWS):
        lo = jnp.clip(pos - win // 2, 0, L)
        hi = jnp.clip(pos + win - win // 2, 0, L)
        sl = slice(g * POOL_GROUP_DIM, (g + 1) * POOL_GROUP_DIM)
        cnt = (hi - lo).astype(jnp.float32)[None, :, None]
        groups.append((cs[:, hi, sl] - cs[:, lo, sl]) / cnt - hf[:, :, sl])
    pooled = jnp.stack(groups, axis=2).astype(h.dtype)
    y = jnp.einsum('blgc,gcd->blgd', pooled, w).reshape(B, L, D)
    return y * scale


def mla_queries(q_a, q_norm_g, w_qb):
    B, L, _ = q_a.shape
    q = (rmsnorm(q_a, q_norm_g) @ w_qb).reshape(B, L, MLA_HEADS, MLA_NOPE + MLA_ROPE)
    return q[..., :MLA_NOPE], q[..., MLA_NOPE:]


def mla_keys_values(kv_a, k_rope, kv_norm_g, w_kvb):
    B, L, _ = kv_a.shape
    kv = (rmsnorm(kv_a, kv_norm_g) @ w_kvb).reshape(B, L, MLA_HEADS, MLA_NOPE + MLA_V)
    k_rope_h = jnp.broadcast_to(k_rope[:, :, None, :], (B, L, MLA_HEADS, MLA_ROPE))
    k = jnp.concatenate([kv[..., :MLA_NOPE], k_rope_h], axis=-1)
    return k, kv[..., MLA_NOPE:]


def mla_mixer(h_lat, h_ctx, w_in, q_norm_g, kv_norm_g, w_qb, w_kvb, w_o, need_ctx_out):
    B, L, _ = h_lat.shape
    Lc = h_ctx.shape[1]
    cos, sin = axial_rope_tables(L)
    a_lat = h_lat @ w_in
    qn, qr = mla_queries(a_lat[..., :MLA_Q_LORA], q_norm_g, w_qb)
    q_lat = jnp.concatenate([qn, apply_rope(qr, cos[:, None, :], sin[:, None, :])], axis=-1)
    k_lat, v_lat = mla_keys_values(a_lat[..., MLA_Q_LORA:MLA_Q_LORA + MLA_KV_LORA],
                                   apply_rope(a_lat[..., MLA_Q_LORA + MLA_KV_LORA:], cos, sin),
                                   kv_norm_g, w_kvb)
    a_ctx_kv = h_ctx @ w_in[:, MLA_Q_LORA:]
    k_ctx, v_ctx = mla_keys_values(a_ctx_kv[..., :MLA_KV_LORA], a_ctx_kv[..., MLA_KV_LORA:], kv_norm_g, w_kvb)
    k_all = jnp.concatenate([k_ctx, k_lat], axis=1)
    v_all = jnp.concatenate([v_ctx, v_lat], axis=1)
    y_lat = blocked_attend(q_lat, k_all, v_all, MLA_SCALE).reshape(B, L, MLA_HEADS * MLA_V) @ w_o
    y_ctx = None
    if need_ctx_out:
        qn_c, qr_c = mla_queries(h_ctx @ w_in[:, :MLA_Q_LORA], q_norm_g, w_qb)
        q_ctx = jnp.concatenate([qn_c, qr_c], axis=-1)
        y_ctx = attend(q_ctx, k_ctx, v_ctx, MLA_SCALE).reshape(B, Lc, MLA_HEADS * MLA_V) @ w_o
    return y_lat, y_ctx


def na_mixer(h_lat, h_ctx, w_in, rpb, w_o, need_ctx_out):
    B, L, _ = h_lat.shape
    Lc = h_ctx.shape[1]
    rows = L // GRID_W
    kr = min(NA_ROWS, rows)
    qkv = (h_lat @ w_in).reshape(B, rows, GRID_W, 3, NA_HEADS, NA_HEAD_DIM)
    q_grid, k_grid, v_grid = qkv[:, :, :, 0], qkv[:, :, :, 1], qkv[:, :, :, 2]
    kv_ctx = (h_ctx @ w_in[:, NA_WIDTH:]).reshape(B, Lc, 2, NA_HEADS, NA_HEAD_DIM)
    k_ctx, v_ctx = kv_ctx[:, :, 0], kv_ctx[:, :, 1]

    cols = jnp.arange(GRID_W)
    col_start = jnp.clip(cols - NA_COLS // 2, 0, GRID_W - NA_COLS)
    col_mask = (cols[None, :] >= col_start[:, None]) & (cols[None, :] < col_start[:, None] + NA_COLS)
    dc_idx = jnp.clip(cols[None, :] - cols[:, None] + NA_COLS - 1, 0, 2 * NA_COLS - 2)
    rpb_f = rpb.astype(jnp.float32)
    row_ids = jnp.arange(rows)
    row_start = jnp.clip(row_ids - kr // 2, 0, rows - kr)

    def row_block(args):
        r, rs, q_r = args
        k_r = lax.dynamic_slice_in_dim(k_grid, rs, kr, axis=1)
        v_r = lax.dynamic_slice_in_dim(v_grid, rs, kr, axis=1)
        dr_idx = rs + jnp.arange(kr) - r + NA_ROWS - 1
        bias = rpb_f[:, dr_idx][:, :, dc_idx].transpose(0, 2, 1, 3)
        s_loc = jnp.einsum('bqhd,bikhd->bhqik', q_r, k_r).astype(jnp.float32) * NA_SCALE + bias
        s_loc = jnp.where(col_mask[:, None, :], s_loc, -jnp.inf)
        s_ctx = jnp.einsum('bqhd,bkhd->bhqk', q_r, k_ctx).astype(jnp.float32) * NA_SCALE
        s = jnp.concatenate([s_loc.reshape(B, NA_HEADS, GRID_W, kr * GRID_W), s_ctx], axis=-1)
        p = jax.nn.softmax(s, axis=-1).astype(v_r.dtype)
        p_loc = p[..., :kr * GRID_W].reshape(B, NA_HEADS, GRID_W, kr, GRID_W)
        p_ctx = p[..., kr * GRID_W:]
        return (jnp.einsum('bhqik,bikhd->bqhd', p_loc, v_r)
                + jnp.einsum('bhqk,bkhd->bqhd', p_ctx, v_ctx))

    o = lax.map(row_block, (row_ids, row_start, q_grid.transpose(1, 0, 2, 3, 4)))
    y_lat = o.transpose(1, 0, 2, 3, 4).reshape(B, L, NA_WIDTH) @ w_o
    y_ctx = None
    if need_ctx_out:
        q_ctx = (h_ctx @ w_in[:, :NA_WIDTH]).reshape(B, Lc, NA_HEADS, NA_HEAD_DIM)
        y_ctx = attend(q_ctx, k_ctx, v_ctx, NA_SCALE).reshape(B, Lc, NA_WIDTH) @ w_o
    return y_lat, y_ctx


def setup_inputs(seed: int = 0) -> dict:
    key = jax.random.key(seed)
    ks = jax.random.split(key, 20)
    D = D_MODEL
    G = POOL_GROUP_DIM

    def nrm(k, shape, scale):
        return jax.random.normal(k, shape, jnp.float32) * scale

    return {
        "x": nrm(ks[0], (BATCH, SEQ, D), 1.0),
        "c": nrm(ks[1], (BATCH, D), 1.0),
        "ctx": nrm(ks[2], (BATCH, CTX_LEN, D), 1.0),
        "c_ctx": nrm(ks[3], (D,), 1.0),
        "ada_w": nrm(ks[4], (DEPTH, D, 6 * D), 0.5 * D ** -0.5),
        "ada_b": nrm(ks[5], (DEPTH, 6 * D), 0.02),
        "norm_g": 1.0 + nrm(ks[6], (DEPTH, 4, D), 0.05),
        "ffn_w_gu": nrm(ks[7], (DEPTH, D, 2 * FFN_HIDDEN), D ** -0.5),
        "ffn_w_down": nrm(ks[8], (DEPTH, FFN_HIDDEN, D), FFN_HIDDEN ** -0.5),
        "pool_w": nrm(ks[9], (N_POOL_LAYERS, POOL_GROUPS, G, G), G ** -0.5),
        "pool_scale": 1.0 + nrm(ks[10], (N_POOL_LAYERS, D), 0.1),
        "mla_w_in": nrm(ks[11], (N_MLA_LAYERS, D, MLA_IN), D ** -0.5),
        "mla_q_norm": 1.0 + nrm(ks[12], (N_MLA_LAYERS, MLA_Q_LORA), 0.05),
        "mla_kv_norm": 1.0 + nrm(ks[13], (N_MLA_LAYERS, MLA_KV_LORA), 0.05),
        "mla_w_qb": nrm(ks[14], (N_MLA_LAYERS, MLA_Q_LORA, MLA_HEADS * (MLA_NOPE + MLA_ROPE)), MLA_Q_LORA ** -0.5),
        "mla_w_kvb": nrm(ks[15], (N_MLA_LAYERS, MLA_KV_LORA, MLA_HEADS * (MLA_NOPE + MLA_V)), MLA_KV_LORA ** -0.5),
        "mla_w_o": nrm(ks[16], (N_MLA_LAYERS, MLA_HEADS * MLA_V, D), (MLA_HEADS * MLA_V) ** -0.5),
        "na_w_in": nrm(ks[17], (N_NA_LAYERS, D, 3 * NA_WIDTH), D ** -0.5),
        "na_rpb": nrm(ks[18], (N_NA_LAYERS, NA_HEADS, 2 * NA_ROWS - 1, 2 * NA_COLS - 1), 0.5),
        "na_w_o": nrm(ks[19], (N_NA_LAYERS, NA_WIDTH, D), NA_WIDTH ** -0.5),
    }


def reference(x, c, ctx, c_ctx, ada_w, ada_b, norm_g, ffn_w_gu, ffn_w_down,
              pool_w, pool_scale, mla_w_in, mla_q_norm, mla_kv_norm, mla_w_qb, mla_w_kvb, mla_w_o,
              na_w_in, na_rpb, na_w_o):
    for i in range(DEPTH):
        kind = i % N_MIXERS
        j = i // N_MIXERS
        ctx_after = any(l % N_MIXERS != MIX_POOL for l in range(i + 1, DEPTH))
        ctx_here = ctx_after or kind != MIX_POOL

        sh_m, sc_m, g_m, sh_f, sc_f, g_f = adaln(c[:, None, :], ada_w[i], ada_b[i])
        h_lat = modulate(rmsnorm(x, norm_g[i, 0]), sh_m, sc_m)
        h_ctx = None
        if ctx_here:
            csh_m, csc_m, cg_m, csh_f, csc_f, cg_f = adaln(c_ctx, ada_w[i], ada_b[i])
            h_ctx = modulate(rmsnorm(ctx, norm_g[i, 0]), csh_m, csc_m)

        if kind == MIX_POOL:
            y_lat = pool_mixer(h_lat, pool_w[j], pool_scale[j])
            y_ctx = pool_mixer(h_ctx, pool_w[j], pool_scale[j]) if ctx_after else None
        elif kind == MIX_MLA:
            y_lat, y_ctx = mla_mixer(h_lat, h_ctx, mla_w_in[j], mla_q_norm[j], mla_kv_norm[j],
                                     mla_w_qb[j], mla_w_kvb[j], mla_w_o[j], ctx_after)
        else:
            y_lat, y_ctx = na_mixer(h_lat, h_ctx, na_w_in[j], na_rpb[j], na_w_o[j], ctx_after)

        x = x + g_m * rmsnorm(y_lat, norm_g[i, 1])
        f_lat = swiglu(modulate(rmsnorm(x, norm_g[i, 2]), sh_f, sc_f), ffn_w_gu[i], ffn_w_down[i])
        x = x + g_f * rmsnorm(f_lat, norm_g[i, 3])

        if ctx_after:
            ctx = ctx + cg_m * rmsnorm(y_ctx, norm_g[i, 1])
            f_ctx = swiglu(modulate(rmsnorm(ctx, norm_g[i, 2]), csh_f, csc_f), ffn_w_gu[i], ffn_w_down[i])
            ctx = ctx + cg_f * rmsnorm(f_ctx, norm_g[i, 3])
    return x
```

```python
import functools

import jax
import jax.numpy as jnp
import numpy as np
from jax import lax
from jax.experimental import pallas as pl
from jax.experimental.pallas import tpu as pltpu

D_MODEL = 1024
DEPTH = 4
GRID_W = 64
N_MIXERS = 3
MIX_POOL, MIX_MLA, MIX_NA = 0, 1, 2
RMS_EPS = 1e-6

POOL_WINDOWS = (2, 4, 8, 16)
POOL_GROUP_DIM = D_MODEL // len(POOL_WINDOWS)
POOL_HALO = 8

MLA_HEADS = 8
MLA_NOPE = 128
MLA_ROPE = 64
MLA_V = 128
MLA_Q_LORA = 384
MLA_KV_LORA = 256
MLA_QK_PAD = 256
MLA_SCALE = (MLA_NOPE + MLA_ROPE) ** -0.5
ROPE_BASE = 10000.0

NA_HEADS = 16
NA_HEAD_DIM = 64
NA_WIDTH = NA_HEADS * NA_HEAD_DIM
NA_ROWS = 8
NA_COLS = 16
NA_SCALE = NA_HEAD_DIM ** -0.5
NA_ROW_BLOCK = 8
NA_WIN_ROWS = 16

FFN_HIDDEN = 2816
FFN_CHUNK = 256

LANES = 128
VMEM_LIMIT_BYTES = 56 * 1024 * 1024

NEG = -1e30

BF16 = jnp.bfloat16
F32 = jnp.float32


def _params(n_axes):
    return pltpu.CompilerParams(
        dimension_semantics=("arbitrary",) * n_axes, vmem_limit_bytes=VMEM_LIMIT_BYTES)


def _const_spec(shape):
    nd = len(shape)
    return pl.BlockSpec(shape, lambda *_: (0,) * nd, pipeline_mode=pl.Buffered(1))


def _rms(xf, g):
    return xf * lax.rsqrt(jnp.mean(xf * xf, axis=-1, keepdims=True) + RMS_EPS) * g


def _mm(a, b):
    return jnp.dot(a.astype(BF16), b, preferred_element_type=F32)


def _mm_nt(a, b):
    return lax.dot_general(a, b, (((1,), (1,)), ((), ())), preferred_element_type=F32)


def _silu(x):
    return x / (1.0 + jnp.exp(-x))


def _ffn(x1, mod, ng_ref, wgu_ref, wd_ref, abuf):
    sh_f, sc_f, g_f = mod[3:4], mod[4:5], mod[5:6]
    h2 = (_rms(x1, ng_ref[2:3]) * (1.0 + sc_f) + sh_f).astype(BF16)
    for c in range(FFN_HIDDEN // FFN_CHUNK):
        lo = c * FFN_CHUNK
        g = jnp.dot(h2, wgu_ref[:, lo:lo + FFN_CHUNK], preferred_element_type=F32)
        u = jnp.dot(h2, wgu_ref[:, FFN_HIDDEN + lo:FFN_HIDDEN + lo + FFN_CHUNK],
                    preferred_element_type=F32)
        abuf[:, lo:lo + FFN_CHUNK] = (_silu(g) * u).astype(BF16)
    f = jnp.dot(abuf[...], wd_ref[...], preferred_element_type=F32)
    return x1 + g_f * _rms(f, ng_ref[3:4])


def _adaln_kernel(cond_ref, w_ref, b_ref, o_ref):
    cond = cond_ref[...]
    o_ref[0] = _mm(_silu(cond), w_ref[0].astype(BF16)) + b_ref[0]


def _adaln(cond, ada_w, ada_b):
    rows = cond.shape[0]
    tn = 1024
    n_out = ada_w.shape[-1]
    return pl.pallas_call(
        _adaln_kernel,
        grid=(DEPTH, n_out // tn),
        in_specs=[
            pl.BlockSpec((rows, D_MODEL), lambda l, n: (0, 0)),
            pl.BlockSpec((1, D_MODEL, tn), lambda l, n: (l, 0, n)),
            pl.BlockSpec((1, 1, tn), lambda l, n: (l, 0, n)),
        ],
        out_specs=pl.BlockSpec((1, rows, tn), lambda l, n: (l, 0, n)),
        out_shape=jax.ShapeDtypeStruct((DEPTH, rows, n_out), F32),
        compiler_params=_params(2),
        name="adaln",
    )(cond, ada_w, ada_b.reshape(DEPTH, 1, n_out))


def _pool_ffn_kernel(x_ref, xp_ref, xn_ref, mod_ref, ng_ref, pw_ref, ps_ref, wgu_ref, wd_ref,
                     o_ref, hbuf, abuf, *, tm, seq_len):
    tiles_per_seq = seq_len // tm
    p0 = (pl.program_id(0) % tiles_per_seq) * tm
    mod = mod_ref[0]
    sh_m, sc_m, g_m = mod[0:1], mod[1:2], mod[2:3]

    def hfun(xv):
        return _rms(xv, ng_ref[0:1]) * (1.0 + sc_m) + sh_m

    x = x_ref[...]
    h_mid = hfun(x)
    hbuf[0:POOL_HALO, :] = jnp.where(p0 > 0, hfun(xp_ref[...]), 0.0)
    hbuf[POOL_HALO:POOL_HALO + tm, :] = h_mid
    hbuf[POOL_HALO + tm:, :] = jnp.where(p0 + tm < seq_len, hfun(xn_ref[...]), 0.0)

    pos = p0 + lax.broadcasted_iota(jnp.int32, (tm, 1), 0)
    ys = []
    for g, win in enumerate(POOL_WINDOWS):
        c0 = g * POOL_GROUP_DIM
        before, after = win // 2, win - win // 2
        s = None
        for d in range(-before, after):
            term = hbuf[POOL_HALO + d:POOL_HALO + d + tm, c0:c0 + POOL_GROUP_DIM]
            s = term if s is None else s + term
        lo = jnp.maximum(pos - before, 0)
        hi = jnp.minimum(pos + after, seq_len)
        inv_cnt = 1.0 / (hi - lo).astype(F32)
        pooled = s * inv_cnt - h_mid[:, c0:c0 + POOL_GROUP_DIM]
        ys.append(_mm(pooled, pw_ref[g]))
    y = jnp.concatenate(ys, axis=-1) * ps_ref[...]
    x1 = x + g_m * _rms(y, ng_ref[1:2])
    o_ref[...] = _ffn(x1, mod, ng_ref, wgu_ref, wd_ref, abuf)


def _pool_ffn(x2d, mod, ng, pw, ps, wgu, wd, *, seq_len, tm):
    t_total = x2d.shape[0]
    n_tiles = t_total // tm
    tiles_per_mod = n_tiles // mod.shape[0]
    halo_blocks = t_total // POOL_HALO
    per_tile = tm // POOL_HALO
    kern = functools.partial(_pool_ffn_kernel, tm=tm, seq_len=seq_len)
    return pl.pallas_call(
        kern,
        grid=(n_tiles,),
        in_specs=[
            pl.BlockSpec((tm, D_MODEL), lambda t: (t, 0)),
            pl.BlockSpec((POOL_HALO, D_MODEL), lambda t: (jnp.maximum(t * per_tile - 1, 0), 0)),
            pl.BlockSpec((POOL_HALO, D_MODEL),
                         lambda t: (jnp.minimum((t + 1) * per_tile, halo_blocks - 1), 0)),
            pl.BlockSpec((1, 6, D_MODEL), lambda t: (t // tiles_per_mod, 0, 0)),
            _const_spec((4, D_MODEL)),
            _const_spec(pw.shape),
            _const_spec((1, D_MODEL)),
            _const_spec(wgu.shape),
            _const_spec(wd.shape),
        ],
        out_specs=pl.BlockSpec((tm, D_MODEL), lambda t: (t, 0)),
        out_shape=jax.ShapeDtypeStruct(x2d.shape, F32),
        scratch_shapes=[
            pltpu.VMEM((tm + 2 * POOL_HALO, D_MODEL), F32),
            pltpu.VMEM((tm, FFN_HIDDEN), BF16),
        ],
        compiler_params=_params(1),
        name="pool_ffn",
    )(x2d, x2d, x2d, mod, ng, pw, ps, wgu, wd)


def _proj_ffn_kernel(x_ref, a_ref, mod_ref, ng_ref, wo_ref, wgu_ref, wd_ref, o_ref, abuf):
    mod = mod_ref[0]
    y = jnp.dot(a_ref[...], wo_ref[...], preferred_element_type=F32)
    x1 = x_ref[...] + mod[2:3] * _rms(y, ng_ref[1:2])
    o_ref[...] = _ffn(x1, mod, ng_ref, wgu_ref, wd_ref, abuf)


def _proj_ffn(x2d, attn2d, mod, ng, wo, wgu, wd, *, tm):
    n_tiles = x2d.shape[0] // tm
    tiles_per_mod = n_tiles // mod.shape[0]
    return pl.pallas_call(
        _proj_ffn_kernel,
        grid=(n_tiles,),
        in_specs=[
            pl.BlockSpec((tm, D_MODEL), lambda t: (t, 0)),
            pl.BlockSpec((tm, attn2d.shape[1]), lambda t: (t, 0)),
            pl.BlockSpec((1, 6, D_MODEL), lambda t: (t // tiles_per_mod, 0, 0)),
            _const_spec((4, D_MODEL)),
            _const_spec(wo.shape),
            _const_spec(wgu.shape),
            _const_spec(wd.shape),
        ],
        out_specs=pl.BlockSpec((tm, D_MODEL), lambda t: (t, 0)),
        out_shape=jax.ShapeDtypeStruct(x2d.shape, F32),
        scratch_shapes=[pltpu.VMEM((tm, FFN_HIDDEN), BF16)],
        compiler_params=_params(1),
        name="proj_ffn",
    )(x2d, attn2d, mod, ng, wo, wgu, wd)


def _norm_proj_kernel(x_ref, mod_ref, ng_ref, w_ref, o_ref, *, n_chunks, chunk):
    mod = mod_ref[0]
    h = (_rms(x_ref[...], ng_ref[0:1]) * (1.0 + mod[1:2]) + mod[0:1]).astype(BF16)
    for c in range(n_chunks):
        sl = slice(c * chunk, (c + 1) * chunk)
        o_ref[:, sl] = jnp.dot(h, w_ref[:, sl], preferred_element_type=F32).astype(BF16)


def _norm_proj(x2d, mod, ng, w, *, tm):
    n_tiles = x2d.shape[0] // tm
    tiles_per_mod = n_tiles // mod.shape[0]
    n_out = w.shape[1]
    chunk = 1024
    kern = functools.partial(_norm_proj_kernel, n_chunks=n_out // chunk, chunk=chunk)
    return pl.pallas_call(
        kern,
        grid=(n_tiles,),
        in_specs=[
            pl.BlockSpec((tm, D_MODEL), lambda t: (t, 0)),
            pl.BlockSpec((1, 6, D_MODEL), lambda t: (t // tiles_per_mod, 0, 0)),
            _const_spec((4, D_MODEL)),
            _const_spec(w.shape),
        ],
        out_specs=pl.BlockSpec((tm, n_out), lambda t: (t, 0)),
        out_shape=jax.ShapeDtypeStruct((x2d.shape[0], n_out), BF16),
        compiler_params=_params(1),
        name="norm_proj",
    )(x2d, mod, ng, w)


def _rope(x, c, sa, sb):
    return x * c + pltpu.roll(x, 32, 1) * sa + pltpu.roll(x, 96, 1) * sb


def _mla_qkv_kernel(x_ref, mod_ref, ng_ref, win_ref, qn_ref, kvn_ref, wqb_ref, wkvb_ref,
                    c_ref, sa_ref, sb_ref, q_ref, k_ref, v_ref):
    mod = mod_ref[0]
    h = _rms(x_ref[0], ng_ref[0:1]) * (1.0 + mod[1:2]) + mod[0:1]
    a = _mm(h, win_ref[...])
    q_pre = _mm(_rms(a[:, :MLA_Q_LORA], qn_ref[...]), wqb_ref[...])
    kv = _mm(_rms(a[:, MLA_Q_LORA:MLA_Q_LORA + MLA_KV_LORA], kvn_ref[...]), wkvb_ref[...])
    c, sa, sb = c_ref[...], sa_ref[...], sb_ref[...]
    k_rope = _rope(a[:, MLA_Q_LORA + MLA_KV_LORA:], c, sa, sb)
    v0 = MLA_HEADS * MLA_NOPE
    for hd in range(MLA_HEADS):
        q0 = hd * MLA_QK_PAD
        q_rope = _rope(q_pre[:, q0 + MLA_NOPE:q0 + MLA_QK_PAD], c, sa, sb)
        q_ref[0, hd] = jnp.concatenate(
            [q_pre[:, q0:q0 + MLA_NOPE], q_rope], axis=-1).astype(BF16)
        k_ref[0, hd] = jnp.concatenate(
            [kv[:, hd * MLA_NOPE:(hd + 1) * MLA_NOPE], k_rope], axis=-1).astype(BF16)
        v_ref[0, hd] = kv[:, v0 + hd * MLA_V:v0 + (hd + 1) * MLA_V].astype(BF16)


def _mla_qkv(x3d, mod, ng, win, qn, kvn, wqb, wkvb, tabs, *, tm):
    b, s, _ = x3d.shape
    per_batch = mod.shape[0] == b
    tab_spec = pl.BlockSpec((tm, LANES), lambda i, t: (t, 0))
    qk_shape = jax.ShapeDtypeStruct((b, MLA_HEADS, s, MLA_QK_PAD), BF16)
    return pl.pallas_call(
        _mla_qkv_kernel,
        grid=(b, s // tm),
        in_specs=[
            pl.BlockSpec((1, tm, D_MODEL), lambda i, t: (i, t, 0)),
            pl.BlockSpec((1, 6, D_MODEL), lambda i, t: (i if per_batch else 0, 0, 0)),
            _const_spec((4, D_MODEL)),
            _const_spec(win.shape),
            _const_spec(qn.shape),
            _const_spec(kvn.shape),
            _const_spec(wqb.shape),
            _const_spec(wkvb.shape),
            tab_spec, tab_spec, tab_spec,
        ],
        out_specs=[
            pl.BlockSpec((1, MLA_HEADS, tm, MLA_QK_PAD), lambda i, t: (i, 0, t, 0)),
            pl.BlockSpec((1, MLA_HEADS, tm, MLA_QK_PAD), lambda i, t: (i, 0, t, 0)),
            pl.BlockSpec((1, MLA_HEADS, tm, MLA_V), lambda i, t: (i, 0, t, 0)),
        ],
        out_shape=[qk_shape, qk_shape, jax.ShapeDtypeStruct((b, MLA_HEADS, s, MLA_V), BF16)],
        compiler_params=_params(2),
        name="mla_qkv",
    )(x3d, mod, ng, win, qn, kvn, wqb, wkvb, *tabs)


def _softmax_chunk(q, k, v, scale):
    s = _mm_nt(q, k)
    m = jnp.max(s, axis=-1, keepdims=True)
    p = jnp.exp((s - m) * scale)
    return m, jnp.sum(p, axis=-1, keepdims=True), jnp.dot(
        p.astype(BF16), v, preferred_element_type=F32)


def _mla_attn_kernel(q_ref, kc_ref, vc_ref, kl_ref, vl_ref, o_ref, m_sc, l_sc, acc_sc,
                     *, kv_chunk, n_chunks):
    q = q_ref[0, 0]
    m0, l0, a0 = _softmax_chunk(q, kc_ref[0, 0], vc_ref[0, 0], MLA_SCALE)
    m_sc[...] = m0
    l_sc[...] = l0
    acc_sc[...] = a0

    def body(c, carry):
        off = pl.multiple_of(c * kv_chunk, kv_chunk)
        k = kl_ref[0, 0, pl.ds(off, kv_chunk), :]
        v = vl_ref[0, 0, pl.ds(off, kv_chunk), :]
        s = _mm_nt(q, k)
        m_old = m_sc[...]
        m_new = jnp.maximum(m_old, jnp.max(s, axis=-1, keepdims=True))
        alpha = jnp.exp((m_old - m_new) * MLA_SCALE)
        p = jnp.exp((s - m_new) * MLA_SCALE)
        l_sc[...] = alpha * l_sc[...] + jnp.sum(p, axis=-1, keepdims=True)
        acc_sc[...] = alpha * acc_sc[...] + jnp.dot(p.astype(BF16), v, preferred_element_type=F32)
        m_sc[...] = m_new
        return carry

    lax.fori_loop(0, n_chunks, body, 0)
    o_ref[0] = (acc_sc[...] / l_sc[...]).astype(BF16)


def _mla_attn(q, kc, vc, kl, vl, *, tq, kv_chunk):
    b, h, l, _ = q.shape
    lc = kc.shape[2]
    kern = functools.partial(_mla_attn_kernel, kv_chunk=kv_chunk, n_chunks=l // kv_chunk)
    return pl.pallas_call(
        kern,
        grid=(b, h, l // tq),
        in_specs=[
            pl.BlockSpec((1, 1, tq, MLA_QK_PAD), lambda i, j, t: (i, j, t, 0)),
            pl.BlockSpec((1, 1, lc, MLA_QK_PAD), lambda i, j, t: (i, j, 0, 0)),
            pl.BlockSpec((1, 1, lc, MLA_V), lambda i, j, t: (i, j, 0, 0)),
            pl.BlockSpec((1, 1, l, MLA_QK_PAD), lambda i, j, t: (i, j, 0, 0)),
            pl.BlockSpec((1, 1, l, MLA_V), lambda i, j, t: (i, j, 0, 0)),
        ],
        out_specs=pl.BlockSpec((1, tq, MLA_V), lambda i, j, t: (i, t, j)),
        out_shape=jax.ShapeDtypeStruct((b, l, h * MLA_V), BF16),
        scratch_shapes=[
            pltpu.VMEM((tq, 1), F32), pltpu.VMEM((tq, 1), F32), pltpu.VMEM((tq, MLA_V), F32)],
        compiler_params=_params(3),
        name="mla_attn",
    )(q, kc, vc, kl, vl)


def _ctx_attn_kernel(q_ref, k_ref, v_ref, o_ref):
    _, l, acc = _softmax_chunk(q_ref[0, 0], k_ref[0, 0], v_ref[0, 0], MLA_SCALE)
    o_ref[0] = (acc / l).astype(BF16)


def _ctx_attn(q, k, v):
    b, h, lc, _ = q.shape
    qk_spec = pl.BlockSpec((1, 1, lc, MLA_QK_PAD), lambda i, j: (i, j, 0, 0))
    return pl.pallas_call(
        _ctx_attn_kernel,
        grid=(b, h),
        in_specs=[qk_spec, qk_spec, pl.BlockSpec((1, 1, lc, MLA_V), lambda i, j: (i, j, 0, 0))],
        out_specs=pl.BlockSpec((1, lc, MLA_V), lambda i, j: (i, 0, j)),
        out_shape=jax.ShapeDtypeStruct((b, lc, h * MLA_V), BF16),
        compiler_params=_params(2),
        name="mla_ctx_attn",
    )(q, k, v)


def _na_attn_kernel(q_ref, k_ref, v_ref, kc_ref, vc_ref, tab_ref, o_ref, s_sc, *, rows):
    j = pl.program_id(2)
    r_first = j * NA_ROW_BLOCK
    win0 = jnp.clip(r_first - NA_ROWS // 2, 0, rows - NA_WIN_ROWS)
    off = pl.multiple_of(win0 * GRID_W, 4 * GRID_W)
    n_loc = NA_WIN_ROWS * GRID_W
    kwin = k_ref[pl.ds(off, n_loc), :]
    vwin = v_ref[pl.ds(off, n_loc), :]
    kc = kc_ref[...]
    vc = vc_ref[...]
    q = q_ref[...]
    lane_head = lax.broadcasted_iota(jnp.int32, (1, LANES), 1) // NA_HEAD_DIM
    key_row = lax.broadcasted_iota(jnp.int32, (1, n_loc), 1) // GRID_W
    outs = []
    for hh in range(2):
        qm = jnp.where(lane_head == hh, q, jnp.zeros_like(q))
        s_loc = _mm_nt(qm, kwin)
        s_ctx = _mm_nt(qm, kc) * NA_SCALE
        for i in range(NA_ROW_BLOCK):
            r = r_first + i
            row_start = jnp.clip(r - NA_ROWS // 2, 0, rows - NA_ROWS)
            k_lo = row_start - win0
            row_mask = jnp.where((key_row >= k_lo) & (key_row < k_lo + NA_ROWS), 0.0, NEG)
            tiles = []
            for kp in range(NA_WIN_ROWS // 2):
                dr0 = win0 + 2 * kp - r + NA_ROWS - 1
                tiles.append(tab_ref[hh, jnp.clip(dr0 + 1, 0, 2 * NA_ROWS - 1)])
            bias = jnp.concatenate(tiles, axis=-1) + row_mask
            s_sc[i * GRID_W:(i + 1) * GRID_W, :] = (
                s_loc[i * GRID_W:(i + 1) * GRID_W, :] * NA_SCALE + bias)
        sl = s_sc[...]
        m = jnp.maximum(jnp.max(sl, axis=-1, keepdims=True),
                        jnp.max(s_ctx, axis=-1, keepdims=True))
        p_loc = jnp.exp(sl - m)
        p_ctx = jnp.exp(s_ctx - m)
        denom = jnp.sum(p_loc, axis=-1, keepdims=True) + jnp.sum(p_ctx, axis=-1, keepdims=True)
        acc = (jnp.dot(p_loc.astype(BF16), vwin, preferred_element_type=F32)
               + jnp.dot(p_ctx.astype(BF16), vc, preferred_element_type=F32))
        outs.append(acc / denom)
    o_ref[...] = jnp.where(lane_head == 0, outs[0], outs[1]).astype(BF16)


def _na_attn(qkv, kvc, tab, *, batch, seq_len, ctx_len):
    rows = seq_len // GRID_W
    n_pairs = NA_WIDTH // LANES
    tq = NA_ROW_BLOCK * GRID_W
    blocks_per_seq = seq_len // tq
    kern = functools.partial(_na_attn_kernel, rows=rows)
    return pl.pallas_call(
        kern,
        grid=(n_pairs, batch, blocks_per_seq),
        in_specs=[
            pl.BlockSpec((tq, LANES), lambda p, b, j: (b * blocks_per_seq + j, p)),
            pl.BlockSpec((seq_len, LANES), lambda p, b, j: (b, n_pairs + p)),
            pl.BlockSpec((seq_len, LANES), lambda p, b, j: (b, 2 * n_pairs + p)),
            pl.BlockSpec((ctx_len, LANES), lambda p, b, j: (b, p)),
            pl.BlockSpec((ctx_len, LANES), lambda p, b, j: (b, n_pairs + p)),
            pl.BlockSpec((2, 2 * NA_ROWS, GRID_W, LANES), lambda p, b, j: (p, 0, 0, 0)),
        ],
        out_specs=pl.BlockSpec((tq, LANES), lambda p, b, j: (b * blocks_per_seq + j, p)),
        out_shape=jax.ShapeDtypeStruct((batch * seq_len, NA_WIDTH), BF16),
        scratch_shapes=[pltpu.VMEM((tq, NA_WIN_ROWS * GRID_W), F32)],
        compiler_params=_params(3),
        name="na_attn",
    )(qkv, qkv, qkv, kvc, kvc, tab)


def _na_bias_table(rpb):
    cols = np.arange(GRID_W)
    col_start = np.clip(cols - NA_COLS // 2, 0, GRID_W - NA_COLS)
    col_mask = (cols[None, :] >= col_start[:, None]) & (cols[None, :] < col_start[:, None] + NA_COLS)
    dc_idx = np.clip(cols[None, :] - cols[:, None] + NA_COLS - 1, 0, 2 * NA_COLS - 2)
    t = jnp.where(col_mask[None, None], rpb.astype(F32)[:, :, dc_idx], NEG)
    zero = jnp.zeros_like(t[:, :1])
    left = jnp.concatenate([zero, t], axis=1)
    right = jnp.concatenate([t, zero], axis=1)
    return jnp.concatenate([left, right], axis=-1)


def _rope_tables(seq_len):
    t = np.arange(seq_len)
    row = (t // GRID_W).astype(np.float32)
    col = (t % GRID_W).astype(np.float32)
    half = MLA_ROPE // 2
    inv = jnp.asarray(ROPE_BASE, F32) ** (-jnp.arange(0, half, 2, dtype=F32) / half)
    ang = jnp.concatenate([row[:, None] * inv, col[:, None] * inv], axis=-1)
    cos, sin = jnp.cos(ang), jnp.sin(ang)
    z32 = jnp.zeros_like(cos)
    z64 = jnp.zeros((seq_len, 64), F32)
    return (jnp.concatenate([cos, cos, z64], axis=-1),
            jnp.concatenate([z32, sin, z64], axis=-1),
            jnp.concatenate([-sin, z32, z64], axis=-1))


def _identity_rope_tables(seq_len):
    ones = jnp.ones((seq_len, 64), F32)
    z = jnp.zeros((seq_len, 64), F32)
    c = jnp.concatenate([ones, z], axis=-1)
    zz = jnp.zeros((seq_len, LANES), F32)
    return c, zz, zz


def kernel(x, c, ctx, c_ctx, ada_w, ada_b, norm_g, ffn_w_gu, ffn_w_down, pool_w, pool_scale,
           mla_w_in, mla_q_norm, mla_kv_norm, mla_w_qb, mla_w_kvb, mla_w_o, na_w_in, na_rpb,
           na_w_o):
    batch, seq_len, d = x.shape
    ctx_len = ctx.shape[1]
    tm = 512

    n_cond = -(-(batch + 1) // 8) * 8
    cond = jnp.zeros((n_cond, d), F32).at[:batch].set(c).at[batch].set(c_ctx)
    mod_all = _adaln(cond, ada_w, ada_b).reshape(DEPTH, n_cond, 6, d)

    wgu = ffn_w_gu.astype(BF16)
    wd = ffn_w_down.astype(BF16)

    xl = x.reshape(batch * seq_len, d)
    xc = ctx.reshape(batch * ctx_len, d)

    for i in range(DEPTH):
        kind = i % N_MIXERS
        j = i // N_MIXERS
        ctx_after = any(l % N_MIXERS != MIX_POOL for l in range(i + 1, DEPTH))
        mod_lat = mod_all[i, :batch]
        mod_ctx = mod_all[i, batch:batch + 1]
        ng = norm_g[i]

        if kind == MIX_POOL:
            pw = pool_w[j].astype(BF16)
            ps = pool_scale[j].reshape(1, d)
            xl = _pool_ffn(xl, mod_lat, ng, pw, ps, wgu[i], wd[i], seq_len=seq_len, tm=tm)
            if ctx_after:
                xc = _pool_ffn(xc, mod_ctx, ng, pw, ps, wgu[i], wd[i], seq_len=ctx_len, tm=ctx_len)
        elif kind == MIX_MLA:
            win = jnp.pad(mla_w_in[j], ((0, 0), (0, 64))).astype(BF16)
            wqb = jnp.pad(
                mla_w_qb[j].reshape(MLA_Q_LORA, MLA_HEADS, MLA_NOPE + MLA_ROPE),
                ((0, 0), (0, 0), (0, MLA_QK_PAD - MLA_NOPE - MLA_ROPE)),
            ).reshape(MLA_Q_LORA, MLA_HEADS * MLA_QK_PAD).astype(BF16)
            wkvb = mla_w_kvb[j].reshape(MLA_KV_LORA, MLA_HEADS, 2, MLA_NOPE).transpose(
                0, 2, 1, 3).reshape(MLA_KV_LORA, 2 * MLA_HEADS * MLA_NOPE).astype(BF16)
            qn = mla_q_norm[j].reshape(1, MLA_Q_LORA)
            kvn = mla_kv_norm[j].reshape(1, MLA_KV_LORA)
            wo = mla_w_o[j].astype(BF16)
            ql, kl, vl = _mla_qkv(xl.reshape(batch, seq_len, d), mod_lat, ng, win, qn, kvn, wqb,
                                  wkvb, _rope_tables(seq_len), tm=tm)
            qc, kc, vc = _mla_qkv(xc.reshape(batch, ctx_len, d), mod_ctx, ng, win, qn, kvn, wqb,
                                  wkvb, _identity_rope_tables(ctx_len), tm=ctx_len)
            a_lat = _mla_attn(ql, kc, vc, kl, vl, tq=256, kv_chunk=512)
            xl = _proj_ffn(xl, a_lat.reshape(batch * seq_len, -1), mod_lat, ng, wo, wgu[i], wd[i],
                           tm=tm)
            if ctx_after:
                a_ctx = _ctx_attn(qc, kc, vc)
                xc = _proj_ffn(xc, a_ctx.reshape(batch * ctx_len, -1), mod_ctx, ng, wo, wgu[i],
                               wd[i], tm=ctx_len)
        else:
            w_in = na_w_in[j].astype(BF16)
            wo = na_w_o[j].astype(BF16)
            qkv = _norm_proj(xl, mod_lat, ng, w_in, tm=tm)
            kvc = _norm_proj(xc, mod_ctx, ng, w_in[:, NA_WIDTH:], tm=ctx_len)
            a_lat = _na_attn(qkv, kvc, _na_bias_table(na_rpb[j]), batch=batch, seq_len=seq_len,
                             ctx_len=ctx_len)
            xl = _proj_ffn(xl, a_lat, mod_lat, ng, wo, wgu[i], wd[i], tm=tm)
            if ctx_after:
                raise NotImplementedError("context output of a neighbourhood layer")
    return xl.reshape(batch, seq_len, d)
```

```python
import functools

import jax
import jax.numpy as jnp
import numpy as np
from jax import lax
from jax.experimental import pallas as pl
from jax.experimental.pallas import tpu as pltpu

D_MODEL = 1024
DEPTH = 4
GRID_W = 64
N_MIXERS = 3
MIX_POOL, MIX_MLA, MIX_NA = 0, 1, 2
RMS_EPS = 1e-6

POOL_WINDOWS = (2, 4, 8, 16)
POOL_GROUP_DIM = D_MODEL // len(POOL_WINDOWS)
POOL_HALO = 8

MLA_HEADS = 8
MLA_NOPE = 128
MLA_ROPE = 64
MLA_V = 128
MLA_Q_LORA = 384
MLA_KV_LORA = 256
MLA_QK_PAD = 256
MLA_SCALE = (MLA_NOPE + MLA_ROPE) ** -0.5
ROPE_BASE = 10000.0

NA_HEADS = 16
NA_HEAD_DIM = 64
NA_WIDTH = NA_HEADS * NA_HEAD_DIM
NA_ROWS = 8
NA_COLS = 16
NA_SCALE = NA_HEAD_DIM ** -0.5
NA_ROW_BLOCK = 8
NA_WIN_ROWS = 16

FFN_HIDDEN = 2816
FFN_CHUNK = 256

LANES = 128
VMEM_LIMIT_BYTES = 56 * 1024 * 1024

NEG = -1e30

BF16 = jnp.bfloat16
F32 = jnp.float32


def _params(n_axes, flags=None):
    return pltpu.CompilerParams(
        dimension_semantics=("arbitrary",) * n_axes, vmem_limit_bytes=VMEM_LIMIT_BYTES,
        flags=flags)


def _const_spec(shape):
    nd = len(shape)
    return pl.BlockSpec(shape, lambda *_: (0,) * nd, pipeline_mode=pl.Buffered(1))


def _rms(xf, g):
    return xf * lax.rsqrt(jnp.mean(xf * xf, axis=-1, keepdims=True) + RMS_EPS) * g


def _mm(a, b):
    return jnp.dot(a.astype(BF16), b, preferred_element_type=F32)


def _mm_nt(a, b):
    return lax.dot_general(a, b, (((1,), (1,)), ((), ())), preferred_element_type=F32)


def _silu(x):
    return x / (1.0 + jnp.exp(-x))


def _ffn(x1, mod, ng_ref, wgu_ref, wd_ref, abuf):
    sh_f, sc_f, g_f = mod[3:4], mod[4:5], mod[5:6]
    h2 = (_rms(x1, ng_ref[2:3]) * (1.0 + sc_f) + sh_f).astype(BF16)
    for c in range(FFN_HIDDEN // FFN_CHUNK):
        lo = c * FFN_CHUNK
        g = jnp.dot(h2, wgu_ref[:, lo:lo + FFN_CHUNK], preferred_element_type=F32)
        u = jnp.dot(h2, wgu_ref[:, FFN_HIDDEN + lo:FFN_HIDDEN + lo + FFN_CHUNK],
                    preferred_element_type=F32)
        abuf[:, lo:lo + FFN_CHUNK] = (_silu(g) * u).astype(BF16)
    f = jnp.dot(abuf[...], wd_ref[...], preferred_element_type=F32)
    return x1 + g_f * _rms(f, ng_ref[3:4])


def _adaln_kernel(cond_ref, w_ref, b_ref, o_ref):
    cond = cond_ref[...]
    o_ref[0] = _mm(_silu(cond), w_ref[0].astype(BF16)) + b_ref[0]


def _adaln(cond, ada_w, ada_b):
    rows = cond.shape[0]
    tn = 1024
    n_out = ada_w.shape[-1]
    return pl.pallas_call(
        _adaln_kernel,
        grid=(DEPTH, n_out // tn),
        in_specs=[
            pl.BlockSpec((rows, D_MODEL), lambda l, n: (0, 0)),
            pl.BlockSpec((1, D_MODEL, tn), lambda l, n: (l, 0, n)),
            pl.BlockSpec((1, 1, tn), lambda l, n: (l, 0, n)),
        ],
        out_specs=pl.BlockSpec((1, rows, tn), lambda l, n: (l, 0, n)),
        out_shape=jax.ShapeDtypeStruct((DEPTH, rows, n_out), F32),
        compiler_params=_params(2),
        name="adaln",
    )(cond, ada_w, ada_b.reshape(DEPTH, 1, n_out))


def _pool_ffn_kernel(x_ref, xp_ref, xn_ref, mod_ref, ng_ref, pw_ref, ps_ref, wgu_ref, wd_ref,
                     o_ref, hbuf, abuf, *, tm, seq_len):
    tiles_per_seq = seq_len // tm
    p0 = (pl.program_id(0) % tiles_per_seq) * tm
    mod = mod_ref[0]
    sh_m, sc_m, g_m = mod[0:1], mod[1:2], mod[2:3]

    def hfun(xv):
        return _rms(xv, ng_ref[0:1]) * (1.0 + sc_m) + sh_m

    x = x_ref[...]
    h_mid = hfun(x)
    hbuf[0:POOL_HALO, :] = jnp.where(p0 > 0, hfun(xp_ref[...]), 0.0)
    hbuf[POOL_HALO:POOL_HALO + tm, :] = h_mid
    hbuf[POOL_HALO + tm:, :] = jnp.where(p0 + tm < seq_len, hfun(xn_ref[...]), 0.0)

    pos = p0 + lax.broadcasted_iota(jnp.int32, (tm, 1), 0)
    ys = []
    for g, win in enumerate(POOL_WINDOWS):
        c0 = g * POOL_GROUP_DIM
        before, after = win // 2, win - win // 2
        s = None
        for d in range(-before, after):
            term = hbuf[POOL_HALO + d:POOL_HALO + d + tm, c0:c0 + POOL_GROUP_DIM]
            s = term if s is None else s + term
        lo = jnp.maximum(pos - before, 0)
        hi = jnp.minimum(pos + after, seq_len)
        inv_cnt = 1.0 / (hi - lo).astype(F32)
        pooled = s * inv_cnt - h_mid[:, c0:c0 + POOL_GROUP_DIM]
        ys.append(_mm(pooled, pw_ref[g]))
    y = jnp.concatenate(ys, axis=-1) * ps_ref[...]
    x1 = x + g_m * _rms(y, ng_ref[1:2])
    o_ref[...] = _ffn(x1, mod, ng_ref, wgu_ref, wd_ref, abuf)


def _pool_ffn(x2d, mod, ng, pw, ps, wgu, wd, *, seq_len, tm):
    t_total = x2d.shape[0]
    n_tiles = t_total // tm
    tiles_per_mod = n_tiles // mod.shape[0]
    halo_blocks = t_total // POOL_HALO
    per_tile = tm // POOL_HALO
    kern = functools.partial(_pool_ffn_kernel, tm=tm, seq_len=seq_len)
    return pl.pallas_call(
        kern,
        grid=(n_tiles,),
        in_specs=[
            pl.BlockSpec((tm, D_MODEL), lambda t: (t, 0)),
            pl.BlockSpec((POOL_HALO, D_MODEL), lambda t: (jnp.maximum(t * per_tile - 1, 0), 0)),
            pl.BlockSpec((POOL_HALO, D_MODEL),
                         lambda t: (jnp.minimum((t + 1) * per_tile, halo_blocks - 1), 0)),
            pl.BlockSpec((1, 6, D_MODEL), lambda t: (t // tiles_per_mod, 0, 0)),
            _const_spec((4, D_MODEL)),
            _const_spec(pw.shape),
            _const_spec((1, D_MODEL)),
            _const_spec(wgu.shape),
            _const_spec(wd.shape),
        ],
        out_specs=pl.BlockSpec((tm, D_MODEL), lambda t: (t, 0)),
        out_shape=jax.ShapeDtypeStruct(x2d.shape, F32),
        scratch_shapes=[
            pltpu.VMEM((tm + 2 * POOL_HALO, D_MODEL), F32),
            pltpu.VMEM((tm, FFN_HIDDEN), BF16),
        ],
        compiler_params=_params(1),
        name="pool_ffn",
    )(x2d, x2d, x2d, mod, ng, pw, ps, wgu, wd)


def _proj_ffn_kernel(x_ref, a_ref, mod_ref, ng_ref, wo_ref, wgu_ref, wd_ref, o_ref, abuf):
    mod = mod_ref[0]
    y = jnp.dot(a_ref[...], wo_ref[...], preferred_element_type=F32)
    x1 = x_ref[...] + mod[2:3] * _rms(y, ng_ref[1:2])
    o_ref[...] = _ffn(x1, mod, ng_ref, wgu_ref, wd_ref, abuf)


def _proj_ffn(x2d, attn2d, mod, ng, wo, wgu, wd, *, tm):
    n_tiles = x2d.shape[0] // tm
    tiles_per_mod = n_tiles // mod.shape[0]
    return pl.pallas_call(
        _proj_ffn_kernel,
        grid=(n_tiles,),
        in_specs=[
            pl.BlockSpec((tm, D_MODEL), lambda t: (t, 0)),
            pl.BlockSpec((tm, attn2d.shape[1]), lambda t: (t, 0)),
            pl.BlockSpec((1, 6, D_MODEL), lambda t: (t // tiles_per_mod, 0, 0)),
            _const_spec((4, D_MODEL)),
            _const_spec(wo.shape),
            _const_spec(wgu.shape),
            _const_spec(wd.shape),
        ],
        out_specs=pl.BlockSpec((tm, D_MODEL), lambda t: (t, 0)),
        out_shape=jax.ShapeDtypeStruct(x2d.shape, F32),
        scratch_shapes=[pltpu.VMEM((tm, FFN_HIDDEN), BF16)],
        compiler_params=_params(1),
        name="proj_ffn",
    )(x2d, attn2d, mod, ng, wo, wgu, wd)


def _norm_proj_kernel(x_ref, mod_ref, ng_ref, w_ref, o_ref, *, n_chunks, chunk):
    mod = mod_ref[0]
    h = (_rms(x_ref[...], ng_ref[0:1]) * (1.0 + mod[1:2]) + mod[0:1]).astype(BF16)
    for c in range(n_chunks):
        sl = slice(c * chunk, (c + 1) * chunk)
        o_ref[:, sl] = jnp.dot(h, w_ref[:, sl], preferred_element_type=F32).astype(BF16)


def _norm_proj(x2d, mod, ng, w, *, tm):
    n_tiles = x2d.shape[0] // tm
    tiles_per_mod = n_tiles // mod.shape[0]
    n_out = w.shape[1]
    chunk = 1024
    kern = functools.partial(_norm_proj_kernel, n_chunks=n_out // chunk, chunk=chunk)
    return pl.pallas_call(
        kern,
        grid=(n_tiles,),
        in_specs=[
            pl.BlockSpec((tm, D_MODEL), lambda t: (t, 0)),
            pl.BlockSpec((1, 6, D_MODEL), lambda t: (t // tiles_per_mod, 0, 0)),
            _const_spec((4, D_MODEL)),
            _const_spec(w.shape),
        ],
        out_specs=pl.BlockSpec((tm, n_out), lambda t: (t, 0)),
        out_shape=jax.ShapeDtypeStruct((x2d.shape[0], n_out), BF16),
        compiler_params=_params(1),
        name="norm_proj",
    )(x2d, mod, ng, w)


def _rope(x, c, sa, sb):
    return x * c + pltpu.roll(x, 32, 1) * sa + pltpu.roll(x, 96, 1) * sb


def _mla_qkv_kernel(x_ref, mod_ref, ng_ref, win_ref, qn_ref, kvn_ref, wqb_ref, wkvb_ref,
                    c_ref, sa_ref, sb_ref, q_ref, k_ref, v_ref):
    mod = mod_ref[0]
    h = _rms(x_ref[0], ng_ref[0:1]) * (1.0 + mod[1:2]) + mod[0:1]
    a = _mm(h, win_ref[...])
    q_pre = _mm(_rms(a[:, :MLA_Q_LORA], qn_ref[...]), wqb_ref[...])
    kv = _mm(_rms(a[:, MLA_Q_LORA:MLA_Q_LORA + MLA_KV_LORA], kvn_ref[...]), wkvb_ref[...])
    c, sa, sb = c_ref[...], sa_ref[...], sb_ref[...]
    k_rope = _rope(a[:, MLA_Q_LORA + MLA_KV_LORA:], c, sa, sb)
    v0 = MLA_HEADS * MLA_NOPE
    for hd in range(MLA_HEADS):
        q0 = hd * MLA_QK_PAD
        q_rope = _rope(q_pre[:, q0 + MLA_NOPE:q0 + MLA_QK_PAD], c, sa, sb)
        q_ref[0, hd] = jnp.concatenate(
            [q_pre[:, q0:q0 + MLA_NOPE], q_rope], axis=-1).astype(BF16)
        k_ref[0, hd] = jnp.concatenate(
            [kv[:, hd * MLA_NOPE:(hd + 1) * MLA_NOPE], k_rope], axis=-1).astype(BF16)
        v_ref[0, hd] = kv[:, v0 + hd * MLA_V:v0 + (hd + 1) * MLA_V].astype(BF16)


def _mla_qkv(x3d, mod, ng, win, qn, kvn, wqb, wkvb, tabs, *, tm):
    b, s, _ = x3d.shape
    per_batch = mod.shape[0] == b
    tab_spec = pl.BlockSpec((tm, LANES), lambda i, t: (t, 0))
    qk_shape = jax.ShapeDtypeStruct((b, MLA_HEADS, s, MLA_QK_PAD), BF16)
    return pl.pallas_call(
        _mla_qkv_kernel,
        grid=(b, s // tm),
        in_specs=[
            pl.BlockSpec((1, tm, D_MODEL), lambda i, t: (i, t, 0)),
            pl.BlockSpec((1, 6, D_MODEL), lambda i, t: (i if per_batch else 0, 0, 0)),
            _const_spec((4, D_MODEL)),
            _const_spec(win.shape),
            _const_spec(qn.shape),
            _const_spec(kvn.shape),
            _const_spec(wqb.shape),
            _const_spec(wkvb.shape),
            tab_spec, tab_spec, tab_spec,
        ],
        out_specs=[
            pl.BlockSpec((1, MLA_HEADS, tm, MLA_QK_PAD), lambda i, t: (i, 0, t, 0)),
            pl.BlockSpec((1, MLA_HEADS, tm, MLA_QK_PAD), lambda i, t: (i, 0, t, 0)),
            pl.BlockSpec((1, MLA_HEADS, tm, MLA_V), lambda i, t: (i, 0, t, 0)),
        ],
        out_shape=[qk_shape, qk_shape, jax.ShapeDtypeStruct((b, MLA_HEADS, s, MLA_V), BF16)],
        compiler_params=_params(2),
        name="mla_qkv",
    )(x3d, mod, ng, win, qn, kvn, wqb, wkvb, *tabs)


def _softmax_chunk(q, k, v, scale):
    s = _mm_nt(q, k)
    m = jnp.max(s, axis=-1, keepdims=True)
    p = jnp.exp((s - m) * scale)
    return m, jnp.sum(p, axis=-1, keepdims=True), jnp.dot(
        p.astype(BF16), v, preferred_element_type=F32)


def _lane_tiles(x):
    return [x[:, t * LANES:(t + 1) * LANES] for t in range(x.shape[1] // LANES)]


def _mla_attn_kernel(q_ref, kc_ref, vc_ref, kl_ref, vl_ref, o_ref, s_0, s_1, m_0, m_1,
                     *, tq, kv_chunk):
    s_sc, m_sc = (s_0, s_1), (m_0, m_1)
    lc, l = kc_ref.shape[2], kl_ref.shape[2]
    n_tiles = l // tq
    exp2_scale = MLA_SCALE * np.log2(np.e)
    chunks = [(kc_ref, vc_ref, c0) for c0 in range(0, lc, kv_chunk)]
    chunks += [(kl_ref, vl_ref, c0) for c0 in range(0, l, kv_chunk)]

    def rows(tile):
        start = tile * tq
        return pl.ds(start if isinstance(start, int) else pl.multiple_of(start, tq), tq)

    def stage(score_tile, softmax_tile, parity):
        if score_tile is not None:
            q = q_ref[0, 0, rows(score_tile), :]
        if softmax_tile is not None:
            m = m_sc[parity][...]
        m_part = l_part = acc = None
        for idx, (k_ref, v_ref, c0) in enumerate(chunks):
            cols = slice(idx * kv_chunk, (idx + 1) * kv_chunk)
            if score_tile is not None:
                s = _mm_nt(q, k_ref[0, 0, c0:c0 + kv_chunk, :])
                s_sc[1 - parity][:, cols] = s
                for tile in _lane_tiles(s):
                    m_part = tile if m_part is None else jnp.maximum(m_part, tile)
            if softmax_tile is not None:
                p = jnp.exp2((s_sc[parity][:, cols] - m) * exp2_scale)
                for tile in _lane_tiles(p):
                    l_part = tile if l_part is None else l_part + tile
                pv = jnp.dot(p.astype(BF16), v_ref[0, 0, c0:c0 + kv_chunk, :],
                             preferred_element_type=F32)
                acc = pv if acc is None else acc + pv
        if score_tile is not None:
            m_sc[1 - parity][...] = jnp.max(m_part, axis=-1, keepdims=True)
        if softmax_tile is not None:
            denom = jnp.sum(l_part, axis=-1, keepdims=True)
            o_ref[0, rows(softmax_tile), :] = (acc / denom).astype(BF16)

    stage(0, None, 1)

    def body(i, carry):
        stage(2 * i + 1, 2 * i, 0)
        stage(2 * i + 2, 2 * i + 1, 1)
        return carry

    lax.fori_loop(0, n_tiles // 2 - 1, body, 0)
    stage(n_tiles - 1, n_tiles - 2, 0)
    stage(None, n_tiles - 1, 1)


def _mla_attn(q, kc, vc, kl, vl, *, tq, kv_chunk):
    b, h, l, _ = q.shape
    lc = kc.shape[2]
    kern = functools.partial(_mla_attn_kernel, tq=tq, kv_chunk=kv_chunk)
    return pl.pallas_call(
        kern,
        grid=(b, h),
        in_specs=[
            pl.BlockSpec((1, 1, l, MLA_QK_PAD), lambda i, j: (i, j, 0, 0)),
            pl.BlockSpec((1, 1, lc, MLA_QK_PAD), lambda i, j: (i, j, 0, 0)),
            pl.BlockSpec((1, 1, lc, MLA_V), lambda i, j: (i, j, 0, 0)),
            pl.BlockSpec((1, 1, l, MLA_QK_PAD), lambda i, j: (i, j, 0, 0)),
            pl.BlockSpec((1, 1, l, MLA_V), lambda i, j: (i, j, 0, 0)),
        ],
        out_specs=pl.BlockSpec((1, l, MLA_V), lambda i, j: (i, 0, j)),
        out_shape=jax.ShapeDtypeStruct((b, l, h * MLA_V), BF16),
        scratch_shapes=[pltpu.VMEM((tq, lc + l), F32)] * 2 + [pltpu.VMEM((tq, 1), F32)] * 2,
        compiler_params=_params(2),
        name="mla_attn",
    )(q, kc, vc, kl, vl)


def _ctx_attn_kernel(q_ref, k_ref, v_ref, o_ref):
    _, l, acc = _softmax_chunk(q_ref[0, 0], k_ref[0, 0], v_ref[0, 0], MLA_SCALE)
    o_ref[0] = (acc / l).astype(BF16)


def _ctx_attn(q, k, v):
    b, h, lc, _ = q.shape
    qk_spec = pl.BlockSpec((1, 1, lc, MLA_QK_PAD), lambda i, j: (i, j, 0, 0))
    return pl.pallas_call(
        _ctx_attn_kernel,
        grid=(b, h),
        in_specs=[qk_spec, qk_spec, pl.BlockSpec((1, 1, lc, MLA_V), lambda i, j: (i, j, 0, 0))],
        out_specs=pl.BlockSpec((1, lc, MLA_V), lambda i, j: (i, 0, j)),
        out_shape=jax.ShapeDtypeStruct((b, lc, h * MLA_V), BF16),
        compiler_params=_params(2),
        name="mla_ctx_attn",
    )(q, k, v)


def _na_attn_kernel(q_ref, k_ref, v_ref, kc_ref, vc_ref, tab_ref, o_ref, s_sc, *, rows):
    j = pl.program_id(2)
    r_first = j * NA_ROW_BLOCK
    win0 = jnp.clip(r_first - NA_ROWS // 2, 0, rows - NA_WIN_ROWS)
    off = pl.multiple_of(win0 * GRID_W, 4 * GRID_W)
    n_loc = NA_WIN_ROWS * GRID_W
    kwin = k_ref[pl.ds(off, n_loc), :]
    vwin = v_ref[pl.ds(off, n_loc), :]
    kc = kc_ref[...]
    vc = vc_ref[...]
    q = q_ref[...]
    lane_head = lax.broadcasted_iota(jnp.int32, (1, LANES), 1) // NA_HEAD_DIM
    key_row = lax.broadcasted_iota(jnp.int32, (1, n_loc), 1) // GRID_W
    outs = []
    for hh in range(2):
        qm = jnp.where(lane_head == hh, q, jnp.zeros_like(q))
        s_loc = _mm_nt(qm, kwin)
        s_ctx = _mm_nt(qm, kc) * NA_SCALE
        for i in range(NA_ROW_BLOCK):
            r = r_first + i
            row_start = jnp.clip(r - NA_ROWS // 2, 0, rows - NA_ROWS)
            k_lo = row_start - win0
            row_mask = jnp.where((key_row >= k_lo) & (key_row < k_lo + NA_ROWS), 0.0, NEG)
            tiles = []
            for kp in range(NA_WIN_ROWS // 2):
                dr0 = win0 + 2 * kp - r + NA_ROWS - 1
                tiles.append(tab_ref[hh, jnp.clip(dr0 + 1, 0, 2 * NA_ROWS - 1)])
            bias = jnp.concatenate(tiles, axis=-1) + row_mask
            s_sc[i * GRID_W:(i + 1) * GRID_W, :] = (
                s_loc[i * GRID_W:(i + 1) * GRID_W, :] * NA_SCALE + bias)
        sl = s_sc[...]
        m = jnp.maximum(jnp.max(sl, axis=-1, keepdims=True),
                        jnp.max(s_ctx, axis=-1, keepdims=True))
        p_loc = jnp.exp(sl - m)
        p_ctx = jnp.exp(s_ctx - m)
        denom = jnp.sum(p_loc, axis=-1, keepdims=True) + jnp.sum(p_ctx, axis=-1, keepdims=True)
        acc = (jnp.dot(p_loc.astype(BF16), vwin, preferred_element_type=F32)
               + jnp.dot(p_ctx.astype(BF16), vc, preferred_element_type=F32))
        outs.append(acc / denom)
    o_ref[...] = jnp.where(lane_head == 0, outs[0], outs[1]).astype(BF16)


def _na_attn(qkv, kvc, tab, *, batch, seq_len, ctx_len):
    rows = seq_len // GRID_W
    n_pairs = NA_WIDTH // LANES
    tq = NA_ROW_BLOCK * GRID_W
    blocks_per_seq = seq_len // tq
    kern = functools.partial(_na_attn_kernel, rows=rows)
    return pl.pallas_call(
        kern,
        grid=(n_pairs, batch, blocks_per_seq),
        in_specs=[
            pl.BlockSpec((tq, LANES), lambda p, b, j: (b * blocks_per_seq + j, p)),
            pl.BlockSpec((seq_len, LANES), lambda p, b, j: (b, n_pairs + p)),
            pl.BlockSpec((seq_len, LANES), lambda p, b, j: (b, 2 * n_pairs + p)),
            pl.BlockSpec((ctx_len, LANES), lambda p, b, j: (b, p)),
            pl.BlockSpec((ctx_len, LANES), lambda p, b, j: (b, n_pairs + p)),
            pl.BlockSpec((2, 2 * NA_ROWS, GRID_W, LANES), lambda p, b, j: (p, 0, 0, 0)),
        ],
        out_specs=pl.BlockSpec((tq, LANES), lambda p, b, j: (b * blocks_per_seq + j, p)),
        out_shape=jax.ShapeDtypeStruct((batch * seq_len, NA_WIDTH), BF16),
        scratch_shapes=[pltpu.VMEM((tq, NA_WIN_ROWS * GRID_W), F32)],
        compiler_params=_params(3),
        name="na_attn",
    )(qkv, qkv, qkv, kvc, kvc, tab)


def _na_bias_table(rpb):
    cols = np.arange(GRID_W)
    col_start = np.clip(cols - NA_COLS // 2, 0, GRID_W - NA_COLS)
    col_mask = (cols[None, :] >= col_start[:, None]) & (cols[None, :] < col_start[:, None] + NA_COLS)
    dc_idx = np.clip(cols[None, :] - cols[:, None] + NA_COLS - 1, 0, 2 * NA_COLS - 2)
    t = jnp.where(col_mask[None, None], rpb.astype(F32)[:, :, dc_idx], NEG)
    zero = jnp.zeros_like(t[:, :1])
    left = jnp.concatenate([zero, t], axis=1)
    right = jnp.concatenate([t, zero], axis=1)
    return jnp.concatenate([left, right], axis=-1)


def _rope_tables(seq_len):
    t = np.arange(seq_len)
    row = (t // GRID_W).astype(np.float32)
    col = (t % GRID_W).astype(np.float32)
    half = MLA_ROPE // 2
    inv = jnp.asarray(ROPE_BASE, F32) ** (-jnp.arange(0, half, 2, dtype=F32) / half)
    ang = jnp.concatenate([row[:, None] * inv, col[:, None] * inv], axis=-1)
    cos, sin = jnp.cos(ang), jnp.sin(ang)
    z32 = jnp.zeros_like(cos)
    z64 = jnp.zeros((seq_len, 64), F32)
    return (jnp.concatenate([cos, cos, z64], axis=-1),
            jnp.concatenate([z32, sin, z64], axis=-1),
            jnp.concatenate([-sin, z32, z64], axis=-1))


def _identity_rope_tables(seq_len):
    ones = jnp.ones((seq_len, 64), F32)
    z = jnp.zeros((seq_len, 64), F32)
    c = jnp.concatenate([ones, z], axis=-1)
    zz = jnp.zeros((seq_len, LANES), F32)
    return c, zz, zz


def kernel(x, c, ctx, c_ctx, ada_w, ada_b, norm_g, ffn_w_gu, ffn_w_down, pool_w, pool_scale,
           mla_w_in, mla_q_norm, mla_kv_norm, mla_w_qb, mla_w_kvb, mla_w_o, na_w_in, na_rpb,
           na_w_o):
    batch, seq_len, d = x.shape
    ctx_len = ctx.shape[1]
    tm = 512

    n_cond = -(-(batch + 1) // 8) * 8
    cond = jnp.zeros((n_cond, d), F32).at[:batch].set(c).at[batch].set(c_ctx)
    mod_all = _adaln(cond, ada_w, ada_b).reshape(DEPTH, n_cond, 6, d)

    wgu = ffn_w_gu.astype(BF16)
    wd = ffn_w_down.astype(BF16)

    xl = x.reshape(batch * seq_len, d)
    xc = ctx.reshape(batch * ctx_len, d)

    for i in range(DEPTH):
        kind = i % N_MIXERS
        j = i // N_MIXERS
        ctx_after = any(l % N_MIXERS != MIX_POOL for l in range(i + 1, DEPTH))
        mod_lat = mod_all[i, :batch]
        mod_ctx = mod_all[i, batch:batch + 1]
        ng = norm_g[i]

        if kind == MIX_POOL:
            pw = pool_w[j].astype(BF16)
            ps = pool_scale[j].reshape(1, d)
            xl = _pool_ffn(xl, mod_lat, ng, pw, ps, wgu[i], wd[i], seq_len=seq_len, tm=tm)
            if ctx_after:
                xc = _pool_ffn(xc, mod_ctx, ng, pw, ps, wgu[i], wd[i], seq_len=ctx_len, tm=ctx_len)
        elif kind == MIX_MLA:
            win = jnp.pad(mla_w_in[j], ((0, 0), (0, 64))).astype(BF16)
            wqb = jnp.pad(
                mla_w_qb[j].reshape(MLA_Q_LORA, MLA_HEADS, MLA_NOPE + MLA_ROPE),
                ((0, 0), (0, 0), (0, MLA_QK_PAD - MLA_NOPE - MLA_ROPE)),
            ).reshape(MLA_Q_LORA, MLA_HEADS * MLA_QK_PAD).astype(BF16)
            wkvb = mla_w_kvb[j].reshape(MLA_KV_LORA, MLA_HEADS, 2, MLA_NOPE).transpose(
                0, 2, 1, 3).reshape(MLA_KV_LORA, 2 * MLA_HEADS * MLA_NOPE).astype(BF16)
            qn = mla_q_norm[j].reshape(1, MLA_Q_LORA)
            kvn = mla_kv_norm[j].reshape(1, MLA_KV_LORA)
            wo = mla_w_o[j].astype(BF16)
            ql, kl, vl = _mla_qkv(xl.reshape(batch, seq_len, d), mod_lat, ng, win, qn, kvn, wqb,
                                  wkvb, _rope_tables(seq_len), tm=tm)
            qc, kc, vc = _mla_qkv(xc.reshape(batch, ctx_len, d), mod_ctx, ng, win, qn, kvn, wqb,
                                  wkvb, _identity_rope_tables(ctx_len), tm=ctx_len)
            a_lat = _mla_attn(ql, kc, vc, kl, vl, tq=256, kv_chunk=256)
            xl = _proj_ffn(xl, a_lat.reshape(batch * seq_len, -1), mod_lat, ng, wo, wgu[i], wd[i],
                           tm=tm)
            if ctx_after:
                a_ctx = _ctx_attn(qc, kc, vc)
                xc = _proj_ffn(xc, a_ctx.reshape(batch * ctx_len, -1), mod_ctx, ng, wo, wgu[i],
                               wd[i], tm=ctx_len)
        else:
            w_in = na_w_in[j].astype(BF16)
            wo = na_w_o[j].astype(BF16)
            qkv = _norm_proj(xl, mod_lat, ng, w_in, tm=tm)
            kvc = _norm_proj(xc, mod_ctx, ng, w_in[:, NA_WIDTH:], tm=ctx_len)
            a_lat = _na_attn(qkv, kvc, _na_bias_table(na_rpb[j]), batch=batch, seq_len=seq_len,
                             ctx_len=ctx_len)
            xl = _proj_ffn(xl, a_lat, mod_lat, ng, wo, wgu[i], wd[i], tm=tm)
            if ctx_after:
                raise NotImplementedError("context output of a neighbourhood layer")
    return xl.reshape(batch, seq_len, d)
```

```python
import functools

import jax
import jax.numpy as jnp
import numpy as np
from jax import lax
from jax.experimental import pallas as pl
from jax.experimental.pallas import tpu as pltpu

D_MODEL = 1024
DEPTH = 4
GRID_W = 64
N_MIXERS = 3
MIX_POOL, MIX_MLA, MIX_NA = 0, 1, 2
RMS_EPS = 1e-6

POOL_WINDOWS = (2, 4, 8, 16)
POOL_GROUP_DIM = D_MODEL // len(POOL_WINDOWS)
POOL_HALO = 8

MLA_HEADS = 8
MLA_NOPE = 128
MLA_ROPE = 64
MLA_V = 128
MLA_Q_LORA = 384
MLA_KV_LORA = 256
MLA_QK_PAD = 256
MLA_SCALE = (MLA_NOPE + MLA_ROPE) ** -0.5
ROPE_BASE = 10000.0

NA_HEADS = 16
NA_HEAD_DIM = 64
NA_WIDTH = NA_HEADS * NA_HEAD_DIM
NA_ROWS = 8
NA_COLS = 16
NA_SCALE = NA_HEAD_DIM ** -0.5
NA_ROW_BLOCK = 4
NA_WIN_ROWS = NA_ROW_BLOCK + NA_ROWS

FFN_HIDDEN = 2816
FFN_CHUNK = 256
FFN_SUB_ROWS = 512

LANES = 128
VMEM_LIMIT_BYTES = 56 * 1024 * 1024

NEG = -1e30
LOG2E = float(np.log2(np.e))

BF16 = jnp.bfloat16
F32 = jnp.float32


def _params(n_axes, flags=None):
    return pltpu.CompilerParams(
        dimension_semantics=("arbitrary",) * n_axes, vmem_limit_bytes=VMEM_LIMIT_BYTES,
        flags=flags)


def _const_spec(shape):
    nd = len(shape)
    return pl.BlockSpec(shape, lambda *_: (0,) * nd, pipeline_mode=pl.Buffered(1))


def _layer_spec(stacked_shape, layer):
    rest = tuple(stacked_shape[1:])
    return pl.BlockSpec((None,) + rest, lambda *_: (layer,) + (0,) * len(rest),
                        pipeline_mode=pl.Buffered(1))


def _rms(xf, g):
    return xf * lax.rsqrt(jnp.mean(xf * xf, axis=-1, keepdims=True) + RMS_EPS) * g


def _mm(a, b):
    return jnp.dot(a.astype(BF16), b, preferred_element_type=F32)


def _mm_nt(a, b):
    return lax.dot_general(a, b, (((1,), (1,)), ((), ())), preferred_element_type=F32)


def _silu(x):
    return x / (1.0 + jnp.exp(-x))


def _ffn(x1, mod, ng_ref, wgu_ref, wd_ref, abuf, rs):
    sh_f, sc_f, g_f = mod[3:4], mod[4:5], mod[5:6]
    h2 = (_rms(x1, ng_ref[2:3]) * (1.0 + sc_f) + sh_f).astype(BF16)
    for c in range(FFN_HIDDEN // FFN_CHUNK):
        lo = c * FFN_CHUNK
        g = jnp.dot(h2, wgu_ref[:, lo:lo + FFN_CHUNK], preferred_element_type=F32)
        u = jnp.dot(h2, wgu_ref[:, FFN_HIDDEN + lo:FFN_HIDDEN + lo + FFN_CHUNK],
                    preferred_element_type=F32)
        abuf[rs, lo:lo + FFN_CHUNK] = (_silu(g) * u).astype(BF16)
    f = jnp.dot(abuf[rs, :], wd_ref[...], preferred_element_type=F32)
    return x1 + g_f * _rms(f, ng_ref[3:4])


def _sub_tiles(tm):
    sub = min(tm, FFN_SUB_ROWS)
    return [slice(r0, r0 + sub) for r0 in range(0, tm, sub)]


def _adaln_kernel(cond_ref, w_ref, b_ref, o_ref):
    cond = cond_ref[...]
    o_ref[0] = _mm(_silu(cond), w_ref[0].astype(BF16)) + b_ref[0]


def _adaln(cond, ada_w, ada_b):
    rows = cond.shape[0]
    tn = 1024
    n_out = ada_w.shape[-1]
    return pl.pallas_call(
        _adaln_kernel,
        grid=(DEPTH, n_out // tn),
        in_specs=[
            pl.BlockSpec((rows, D_MODEL), lambda l, n: (0, 0)),
            pl.BlockSpec((1, D_MODEL, tn), lambda l, n: (l, 0, n)),
            pl.BlockSpec((1, 1, tn), lambda l, n: (l, 0, n)),
        ],
        out_specs=pl.BlockSpec((1, rows, tn), lambda l, n: (l, 0, n)),
        out_shape=jax.ShapeDtypeStruct((DEPTH, rows, n_out), F32),
        compiler_params=_params(2),
        name="adaln",
    )(cond, ada_w, ada_b.reshape(DEPTH, 1, n_out))


def _pool_ffn_kernel(x_ref, xp_ref, xn_ref, mod_ref, ng_ref, pw_ref, ps_ref, wgu_ref, wd_ref,
                     o_ref, hbuf, abuf, *, tm, seq_len):
    tiles_per_seq = seq_len // tm
    p0 = (pl.program_id(0) % tiles_per_seq) * tm
    mod = mod_ref[0]
    sh_m, sc_m, g_m = mod[0:1], mod[1:2], mod[2:3]

    def hfun(xv):
        return _rms(xv, ng_ref[0:1]) * (1.0 + sc_m) + sh_m

    hbuf[0:POOL_HALO, :] = jnp.where(p0 > 0, hfun(xp_ref[...]), 0.0)
    hbuf[POOL_HALO:POOL_HALO + tm, :] = hfun(x_ref[...])
    hbuf[POOL_HALO + tm:, :] = jnp.where(p0 + tm < seq_len, hfun(xn_ref[...]), 0.0)

    for rs in _sub_tiles(tm):
        n_rows = rs.stop - rs.start
        base = POOL_HALO + rs.start
        pos = p0 + rs.start + lax.broadcasted_iota(jnp.int32, (n_rows, 1), 0)
        ys = []
        for g, win in enumerate(POOL_WINDOWS):
            cols = slice(g * POOL_GROUP_DIM, (g + 1) * POOL_GROUP_DIM)
            before, after = win // 2, win - win // 2
            s = None
            for d in range(-before, after):
                term = hbuf[base + d:base + d + n_rows, cols]
                s = term if s is None else s + term
            lo = jnp.maximum(pos - before, 0)
            hi = jnp.minimum(pos + after, seq_len)
            inv_cnt = 1.0 / (hi - lo).astype(F32)
            pooled = s * inv_cnt - hbuf[base:base + n_rows, cols]
            ys.append(_mm(pooled, pw_ref[g]))
        y = jnp.concatenate(ys, axis=-1) * ps_ref[...]
        x1 = x_ref[rs, :] + g_m * _rms(y, ng_ref[1:2])
        o_ref[rs, :] = _ffn(x1, mod, ng_ref, wgu_ref, wd_ref, abuf, rs)


def _pool_ffn(x2d, mod, ng, pw, ps, wgu, wd, layer, *, seq_len, tm):
    t_total = x2d.shape[0]
    n_tiles = t_total // tm
    tiles_per_mod = n_tiles // mod.shape[0]
    halo_blocks = t_total // POOL_HALO
    per_tile = tm // POOL_HALO
    kern = functools.partial(_pool_ffn_kernel, tm=tm, seq_len=seq_len)
    return pl.pallas_call(
        kern,
        grid=(n_tiles,),
        in_specs=[
            pl.BlockSpec((tm, D_MODEL), lambda t: (t, 0)),
            pl.BlockSpec((POOL_HALO, D_MODEL), lambda t: (jnp.maximum(t * per_tile - 1, 0), 0)),
            pl.BlockSpec((POOL_HALO, D_MODEL),
                         lambda t: (jnp.minimum((t + 1) * per_tile, halo_blocks - 1), 0)),
            pl.BlockSpec((1, 6, D_MODEL), lambda t: (t // tiles_per_mod, 0, 0)),
            _const_spec((4, D_MODEL)),
            _const_spec(pw.shape),
            _const_spec((1, D_MODEL)),
            _layer_spec(wgu.shape, layer),
            _layer_spec(wd.shape, layer),
        ],
        out_specs=pl.BlockSpec((tm, D_MODEL), lambda t: (t, 0)),
        out_shape=jax.ShapeDtypeStruct(x2d.shape, F32),
        scratch_shapes=[
            pltpu.VMEM((tm + 2 * POOL_HALO, D_MODEL), F32),
            pltpu.VMEM((tm, FFN_HIDDEN), BF16),
        ],
        compiler_params=_params(1),
        name="pool_ffn",
    )(x2d, x2d, x2d, mod, ng, pw, ps, wgu, wd)


def _proj_ffn_kernel(x_ref, a_ref, mod_ref, ng_ref, wo_ref, wgu_ref, wd_ref, o_ref, abuf):
    mod = mod_ref[0]
    for rs in _sub_tiles(x_ref.shape[0]):
        y = jnp.dot(a_ref[rs, :], wo_ref[...], preferred_element_type=F32)
        x1 = x_ref[rs, :] + mod[2:3] * _rms(y, ng_ref[1:2])
        o_ref[rs, :] = _ffn(x1, mod, ng_ref, wgu_ref, wd_ref, abuf, rs)


def _proj_ffn(x2d, attn2d, mod, ng, wo, wgu, wd, layer, *, tm):
    n_tiles = x2d.shape[0] // tm
    tiles_per_mod = n_tiles // mod.shape[0]
    return pl.pallas_call(
        _proj_ffn_kernel,
        grid=(n_tiles,),
        in_specs=[
            pl.BlockSpec((tm, D_MODEL), lambda t: (t, 0)),
            pl.BlockSpec((tm, attn2d.shape[1]), lambda t: (t, 0)),
            pl.BlockSpec((1, 6, D_MODEL), lambda t: (t // tiles_per_mod, 0, 0)),
            _const_spec((4, D_MODEL)),
            _const_spec(wo.shape),
            _layer_spec(wgu.shape, layer),
            _layer_spec(wd.shape, layer),
        ],
        out_specs=pl.BlockSpec((tm, D_MODEL), lambda t: (t, 0)),
        out_shape=jax.ShapeDtypeStruct(x2d.shape, F32),
        scratch_shapes=[pltpu.VMEM((tm, FFN_HIDDEN), BF16)],
        compiler_params=_params(1),
        name="proj_ffn",
    )(x2d, attn2d, mod, ng, wo, wgu, wd)


def _norm_proj_kernel(x_ref, mod_ref, ng_ref, w_ref, o_ref, *, n_chunks, chunk):
    mod = mod_ref[0]
    h = (_rms(x_ref[...], ng_ref[0:1]) * (1.0 + mod[1:2]) + mod[0:1]).astype(BF16)
    for c in range(n_chunks):
        sl = slice(c * chunk, (c + 1) * chunk)
        o_ref[:, sl] = jnp.dot(h, w_ref[:, sl], preferred_element_type=F32).astype(BF16)


def _norm_proj(x2d, mod, ng, w, *, tm):
    n_tiles = x2d.shape[0] // tm
    tiles_per_mod = n_tiles // mod.shape[0]
    n_out = w.shape[1]
    chunk = 1024
    kern = functools.partial(_norm_proj_kernel, n_chunks=n_out // chunk, chunk=chunk)
    return pl.pallas_call(
        kern,
        grid=(n_tiles,),
        in_specs=[
            pl.BlockSpec((tm, D_MODEL), lambda t: (t, 0)),
            pl.BlockSpec((1, 6, D_MODEL), lambda t: (t // tiles_per_mod, 0, 0)),
            _const_spec((4, D_MODEL)),
            _const_spec(w.shape),
        ],
        out_specs=pl.BlockSpec((tm, n_out), lambda t: (t, 0)),
        out_shape=jax.ShapeDtypeStruct((x2d.shape[0], n_out), BF16),
        compiler_params=_params(1),
        name="norm_proj",
    )(x2d, mod, ng, w)


def _rope(x, c, sa, sb):
    return x * c + pltpu.roll(x, 32, 1) * sa + pltpu.roll(x, 96, 1) * sb


def _mla_qkv_kernel(x_ref, mod_ref, ng_ref, win_ref, qn_ref, kvn_ref, wqb_ref, wkvb_ref,
                    c_ref, sa_ref, sb_ref, q_ref, k_ref, v_ref):
    mod = mod_ref[0]
    h = _rms(x_ref[0], ng_ref[0:1]) * (1.0 + mod[1:2]) + mod[0:1]
    a = _mm(h, win_ref[...])
    q_pre = _mm(_rms(a[:, :MLA_Q_LORA], qn_ref[...]), wqb_ref[...])
    kv = _mm(_rms(a[:, MLA_Q_LORA:MLA_Q_LORA + MLA_KV_LORA], kvn_ref[...]), wkvb_ref[...])
    c, sa, sb = c_ref[...], sa_ref[...], sb_ref[...]
    k_rope = _rope(a[:, MLA_Q_LORA + MLA_KV_LORA:], c, sa, sb)
    v0 = MLA_HEADS * MLA_NOPE
    for hd in range(MLA_HEADS):
        q0 = hd * MLA_QK_PAD
        q_rope = _rope(q_pre[:, q0 + MLA_NOPE:q0 + MLA_QK_PAD], c, sa, sb)
        q_ref[0, hd] = jnp.concatenate(
            [q_pre[:, q0:q0 + MLA_NOPE], q_rope], axis=-1).astype(BF16)
        k_ref[0, hd] = jnp.concatenate(
            [kv[:, hd * MLA_NOPE:(hd + 1) * MLA_NOPE], k_rope], axis=-1).astype(BF16)
        v_ref[0, hd] = kv[:, v0 + hd * MLA_V:v0 + (hd + 1) * MLA_V].astype(BF16)


def _mla_qkv(x3d, mod, ng, win, qn, kvn, wqb, wkvb, tabs, *, tm):
    b, s, _ = x3d.shape
    per_batch = mod.shape[0] == b
    tab_spec = pl.BlockSpec((tm, LANES), lambda i, t: (t, 0))
    qk_shape = jax.ShapeDtypeStruct((b, MLA_HEADS, s, MLA_QK_PAD), BF16)
    return pl.pallas_call(
        _mla_qkv_kernel,
        grid=(b, s // tm),
        in_specs=[
            pl.BlockSpec((1, tm, D_MODEL), lambda i, t: (i, t, 0)),
            pl.BlockSpec((1, 6, D_MODEL), lambda i, t: (i if per_batch else 0, 0, 0)),
            _const_spec((4, D_MODEL)),
            _const_spec(win.shape),
            _const_spec(qn.shape),
            _const_spec(kvn.shape),
            _const_spec(wqb.shape),
            _const_spec(wkvb.shape),
            tab_spec, tab_spec, tab_spec,
        ],
        out_specs=[
            pl.BlockSpec((1, MLA_HEADS, tm, MLA_QK_PAD), lambda i, t: (i, 0, t, 0)),
            pl.BlockSpec((1, MLA_HEADS, tm, MLA_QK_PAD), lambda i, t: (i, 0, t, 0)),
            pl.BlockSpec((1, MLA_HEADS, tm, MLA_V), lambda i, t: (i, 0, t, 0)),
        ],
        out_shape=[qk_shape, qk_shape, jax.ShapeDtypeStruct((b, MLA_HEADS, s, MLA_V), BF16)],
        compiler_params=_params(2),
        name="mla_qkv",
    )(x3d, mod, ng, win, qn, kvn, wqb, wkvb, *tabs)


def _softmax_chunk(q, k, v, scale):
    s = _mm_nt(q, k)
    m = jnp.max(s, axis=-1, keepdims=True)
    p = jnp.exp((s - m) * scale)
    return m, jnp.sum(p, axis=-1, keepdims=True), jnp.dot(
        p.astype(BF16), v, preferred_element_type=F32)


def _lane_tiles(x):
    return [x[:, t * LANES:(t + 1) * LANES] for t in range(x.shape[1] // LANES)]


def _mla_attn_kernel(q_ref, kc_ref, vc_ref, kl_ref, vl_ref, o_ref, s_0, s_1, m_0, m_1,
                     *, tq, kv_chunk):
    s_sc, m_sc = (s_0, s_1), (m_0, m_1)
    lc, l = kc_ref.shape[2], kl_ref.shape[2]
    n_tiles = l // tq
    exp2_scale = MLA_SCALE * np.log2(np.e)
    chunks = [(kc_ref, vc_ref, c0) for c0 in range(0, lc, kv_chunk)]
    chunks += [(kl_ref, vl_ref, c0) for c0 in range(0, l, kv_chunk)]

    def rows(tile):
        start = tile * tq
        return pl.ds(start if isinstance(start, int) else pl.multiple_of(start, tq), tq)

    def stage(score_tile, softmax_tile, parity):
        if score_tile is not None:
            q = q_ref[0, 0, rows(score_tile), :]
        if softmax_tile is not None:
            m = m_sc[parity][...]
        m_part = l_part = acc = None
        for idx, (k_ref, v_ref, c0) in enumerate(chunks):
            cols = slice(idx * kv_chunk, (idx + 1) * kv_chunk)
            if score_tile is not None:
                s = _mm_nt(q, k_ref[0, 0, c0:c0 + kv_chunk, :])
                s_sc[1 - parity][:, cols] = s
                for tile in _lane_tiles(s):
                    m_part = tile if m_part is None else jnp.maximum(m_part, tile)
            if softmax_tile is not None:
                p = jnp.exp2((s_sc[parity][:, cols] - m) * exp2_scale)
                for tile in _lane_tiles(p):
                    l_part = tile if l_part is None else l_part + tile
                pv = jnp.dot(p.astype(BF16), v_ref[0, 0, c0:c0 + kv_chunk, :],
                             preferred_element_type=F32)
                acc = pv if acc is None else acc + pv
        if score_tile is not None:
            m_sc[1 - parity][...] = jnp.max(m_part, axis=-1, keepdims=True)
        if softmax_tile is not None:
            denom = jnp.sum(l_part, axis=-1, keepdims=True)
            o_ref[0, rows(softmax_tile), :] = (acc / denom).astype(BF16)

    stage(0, None, 1)

    def body(i, carry):
        stage(2 * i + 1, 2 * i, 0)
        stage(2 * i + 2, 2 * i + 1, 1)
        return carry

    lax.fori_loop(0, n_tiles // 2 - 1, body, 0)
    stage(n_tiles - 1, n_tiles - 2, 0)
    stage(None, n_tiles - 1, 1)


def _mla_attn(q, kc, vc, kl, vl, *, tq, kv_chunk):
    b, h, l, _ = q.shape
    lc = kc.shape[2]
    kern = functools.partial(_mla_attn_kernel, tq=tq, kv_chunk=kv_chunk)
    return pl.pallas_call(
        kern,
        grid=(b, h),
        in_specs=[
            pl.BlockSpec((1, 1, l, MLA_QK_PAD), lambda i, j: (i, j, 0, 0)),
            pl.BlockSpec((1, 1, lc, MLA_QK_PAD), lambda i, j: (i, j, 0, 0)),
            pl.BlockSpec((1, 1, lc, MLA_V), lambda i, j: (i, j, 0, 0)),
            pl.BlockSpec((1, 1, l, MLA_QK_PAD), lambda i, j: (i, j, 0, 0)),
            pl.BlockSpec((1, 1, l, MLA_V), lambda i, j: (i, j, 0, 0)),
        ],
        out_specs=pl.BlockSpec((1, l, MLA_V), lambda i, j: (i, 0, j)),
        out_shape=jax.ShapeDtypeStruct((b, l, h * MLA_V), BF16),
        scratch_shapes=[pltpu.VMEM((tq, lc + l), F32)] * 2 + [pltpu.VMEM((tq, 1), F32)] * 2,
        compiler_params=_params(2),
        name="mla_attn",
    )(q, kc, vc, kl, vl)


def _ctx_attn_kernel(q_ref, k_ref, v_ref, o_ref):
    _, l, acc = _softmax_chunk(q_ref[0, 0], k_ref[0, 0], v_ref[0, 0], MLA_SCALE)
    o_ref[0] = (acc / l).astype(BF16)


def _ctx_attn(q, k, v):
    b, h, lc, _ = q.shape
    qk_spec = pl.BlockSpec((1, 1, lc, MLA_QK_PAD), lambda i, j: (i, j, 0, 0))
    return pl.pallas_call(
        _ctx_attn_kernel,
        grid=(b, h),
        in_specs=[qk_spec, qk_spec, pl.BlockSpec((1, 1, lc, MLA_V), lambda i, j: (i, j, 0, 0))],
        out_specs=pl.BlockSpec((1, lc, MLA_V), lambda i, j: (i, 0, j)),
        out_shape=jax.ShapeDtypeStruct((b, lc, h * MLA_V), BF16),
        compiler_params=_params(2),
        name="mla_ctx_attn",
    )(q, k, v)


def _na_attn_kernel(q_ref, k_ref, v_ref, kc_ref, vc_ref, tab_ref, o_ref,
                    vx_sc, vcx_sc, z_0, z_1, m_0, m_1, *, rows):
    z_sc, m_sc = (z_0, z_1), (m_0, m_1)
    tq = NA_ROW_BLOCK * GRID_W
    n_loc = NA_WIN_ROWS * GRID_W
    n_blocks = rows // NA_ROW_BLOCK
    exp2_scale = NA_SCALE * LOG2E
    lane_head = lax.broadcasted_iota(jnp.int32, (1, LANES), 1) // NA_HEAD_DIM
    key_row = lax.broadcasted_iota(jnp.int32, (1, n_loc), 1) // GRID_W

    vx_sc[:, :LANES] = v_ref[...]
    vx_sc[:, LANES:] = jnp.ones(v_ref.shape, BF16)
    vcx_sc[:, :LANES] = vc_ref[...]
    vcx_sc[:, LANES:] = jnp.ones(vc_ref.shape, BF16)

    def window(j):
        win0 = jnp.clip(j * NA_ROW_BLOCK - NA_ROWS // 2, 0, rows - NA_WIN_ROWS)
        return win0, pl.ds(pl.multiple_of(win0 * GRID_W, GRID_W), n_loc)

    def q_rows(j):
        start = j * tq
        return pl.ds(start if isinstance(start, int) else pl.multiple_of(start, tq), tq)

    def scores(j, slot):
        win0, key_rows = window(j)
        q = q_ref[q_rows(j), :]
        zero = jnp.zeros_like(q)
        q2 = jnp.concatenate(
            [jnp.where(lane_head == 0, q, zero), jnp.where(lane_head == 1, q, zero)], axis=0)
        s_loc = _mm_nt(q2, k_ref[key_rows, :])
        z_ctx = _mm_nt(q2, kc_ref[...]) * exp2_scale
        z_sc[slot][:, n_loc:] = z_ctx
        m_ctx = jnp.max(z_ctx, axis=-1, keepdims=True)
        for hh in range(2):
            for i in range(NA_ROW_BLOCK):
                r = j * NA_ROW_BLOCK + i
                k_lo = jnp.clip(r - NA_ROWS // 2, 0, rows - NA_ROWS) - win0
                row_mask = jnp.where((key_row >= k_lo) & (key_row < k_lo + NA_ROWS), 0.0, NEG)
                tiles = []
                for kp in range(NA_WIN_ROWS // 2):
                    dr0 = win0 + 2 * kp - r + NA_ROWS - 1
                    tiles.append(tab_ref[hh, jnp.clip(dr0 + 1, 0, 2 * NA_ROWS - 1)])
                band = slice(hh * tq + i * GRID_W, hh * tq + (i + 1) * GRID_W)
                z = s_loc[band, :] * exp2_scale + (jnp.concatenate(tiles, axis=-1) + row_mask)
                z_sc[slot][band, :n_loc] = z
                m_sc[slot][band, :] = jnp.maximum(
                    jnp.max(z, axis=-1, keepdims=True), m_ctx[band, :])

    def softmax_values(j, slot):
        _, key_rows = window(j)
        p = jnp.exp2(z_sc[slot][...] - m_sc[slot][...]).astype(BF16)
        acc = (jnp.dot(p[:, :n_loc], vx_sc[key_rows, :], preferred_element_type=F32)
               + jnp.dot(p[:, n_loc:], vcx_sc[...], preferred_element_type=F32))
        out = acc[:, :LANES] / acc[:, LANES:]
        o_ref[q_rows(j), :] = jnp.where(lane_head == 0, out[:tq], out[tq:]).astype(BF16)

    scores(0, 0)

    def body(t, carry):
        scores(2 * t + 1, 1)
        softmax_values(2 * t, 0)
        scores(2 * t + 2, 0)
        softmax_values(2 * t + 1, 1)
        return carry

    lax.fori_loop(0, n_blocks // 2 - 1, body, 0)
    scores(n_blocks - 1, 1)
    softmax_values(n_blocks - 2, 0)
    softmax_values(n_blocks - 1, 1)


def _na_attn(qkv, kvc, tab, *, batch, seq_len, ctx_len):
    rows = seq_len // GRID_W
    n_pairs = NA_WIDTH // LANES
    tq = NA_ROW_BLOCK * GRID_W
    n_keys = NA_WIN_ROWS * GRID_W + ctx_len
    kern = functools.partial(_na_attn_kernel, rows=rows)
    return pl.pallas_call(
        kern,
        grid=(n_pairs, batch),
        in_specs=[
            pl.BlockSpec((seq_len, LANES), lambda p, b: (b, p)),
            pl.BlockSpec((seq_len, LANES), lambda p, b: (b, n_pairs + p)),
            pl.BlockSpec((seq_len, LANES), lambda p, b: (b, 2 * n_pairs + p)),
            pl.BlockSpec((ctx_len, LANES), lambda p, b: (b, p)),
            pl.BlockSpec((ctx_len, LANES), lambda p, b: (b, n_pairs + p)),
            pl.BlockSpec((2, 2 * NA_ROWS, GRID_W, LANES), lambda p, b: (p, 0, 0, 0)),
        ],
        out_specs=pl.BlockSpec((seq_len, LANES), lambda p, b: (b, p)),
        out_shape=jax.ShapeDtypeStruct((batch * seq_len, NA_WIDTH), BF16),
        scratch_shapes=(
            [pltpu.VMEM((seq_len, 2 * LANES), BF16), pltpu.VMEM((ctx_len, 2 * LANES), BF16)]
            + [pltpu.VMEM((2 * tq, n_keys), F32)] * 2 + [pltpu.VMEM((2 * tq, 1), F32)] * 2),
        compiler_params=_params(2),
        name="na_attn",
    )(qkv, qkv, qkv, kvc, kvc, tab)


def _na_bias_table(rpb):
    cols = np.arange(GRID_W)
    col_start = np.clip(cols - NA_COLS // 2, 0, GRID_W - NA_COLS)
    col_mask = (cols[None, :] >= col_start[:, None]) & (cols[None, :] < col_start[:, None] + NA_COLS)
    dc_idx = np.clip(cols[None, :] - cols[:, None] + NA_COLS - 1, 0, 2 * NA_COLS - 2)
    t = jnp.where(col_mask[None, None], rpb.astype(F32)[:, :, dc_idx] * LOG2E, NEG)
    zero = jnp.zeros_like(t[:, :1])
    left = jnp.concatenate([zero, t], axis=1)
    right = jnp.concatenate([t, zero], axis=1)
    return jnp.concatenate([left, right], axis=-1)


def _rope_tables(seq_len):
    t = np.arange(seq_len)
    row = (t // GRID_W).astype(np.float32)
    col = (t % GRID_W).astype(np.float32)
    half = MLA_ROPE // 2
    inv = jnp.asarray(ROPE_BASE, F32) ** (-jnp.arange(0, half, 2, dtype=F32) / half)
    ang = jnp.concatenate([row[:, None] * inv, col[:, None] * inv], axis=-1)
    cos, sin = jnp.cos(ang), jnp.sin(ang)
    z32 = jnp.zeros_like(cos)
    z64 = jnp.zeros((seq_len, 64), F32)
    return (jnp.concatenate([cos, cos, z64], axis=-1),
            jnp.concatenate([z32, sin, z64], axis=-1),
            jnp.concatenate([-sin, z32, z64], axis=-1))


def _identity_rope_tables(seq_len):
    ones = jnp.ones((seq_len, 64), F32)
    z = jnp.zeros((seq_len, 64), F32)
    c = jnp.concatenate([ones, z], axis=-1)
    zz = jnp.zeros((seq_len, LANES), F32)
    return c, zz, zz


def kernel(x, c, ctx, c_ctx, ada_w, ada_b, norm_g, ffn_w_gu, ffn_w_down, pool_w, pool_scale,
           mla_w_in, mla_q_norm, mla_kv_norm, mla_w_qb, mla_w_kvb, mla_w_o, na_w_in, na_rpb,
           na_w_o):
    batch, seq_len, d = x.shape
    ctx_len = ctx.shape[1]
    tm = 512

    n_cond = -(-(batch + 1) // 8) * 8
    cond = jnp.zeros((n_cond, d), F32).at[:batch].set(c).at[batch].set(c_ctx)
    mod_all = _adaln(cond, ada_w, ada_b).reshape(DEPTH, n_cond, 6, d)

    wgu = ffn_w_gu.astype(BF16)
    wd = ffn_w_down.astype(BF16)

    xl = x.reshape(batch * seq_len, d)
    xc = ctx.reshape(batch * ctx_len, d)

    for i in range(DEPTH):
        kind = i % N_MIXERS
        j = i // N_MIXERS
        ctx_after = any(l % N_MIXERS != MIX_POOL for l in range(i + 1, DEPTH))
        mod_lat = mod_all[i, :batch]
        mod_ctx = mod_all[i, batch:batch + 1]
        ng = norm_g[i]

        if kind == MIX_POOL:
            pw = pool_w[j].astype(BF16)
            ps = pool_scale[j].reshape(1, d)
            xl = _pool_ffn(xl, mod_lat, ng, pw, ps, wgu, wd, i, seq_len=seq_len, tm=tm)
            if ctx_after:
                xc = _pool_ffn(xc, mod_ctx, ng, pw, ps, wgu, wd, i, seq_len=ctx_len, tm=ctx_len)
        elif kind == MIX_MLA:
            win = jnp.pad(mla_w_in[j], ((0, 0), (0, 64))).astype(BF16)
            wqb = jnp.pad(
                mla_w_qb[j].reshape(MLA_Q_LORA, MLA_HEADS, MLA_NOPE + MLA_ROPE),
                ((0, 0), (0, 0), (0, MLA_QK_PAD - MLA_NOPE - MLA_ROPE)),
            ).reshape(MLA_Q_LORA, MLA_HEADS * MLA_QK_PAD).astype(BF16)
            wkvb = mla_w_kvb[j].reshape(MLA_KV_LORA, MLA_HEADS, 2, MLA_NOPE).transpose(
                0, 2, 1, 3).reshape(MLA_KV_LORA, 2 * MLA_HEADS * MLA_NOPE).astype(BF16)
            qn = mla_q_norm[j].reshape(1, MLA_Q_LORA)
            kvn = mla_kv_norm[j].reshape(1, MLA_KV_LORA)
            wo = mla_w_o[j].astype(BF16)
            ql, kl, vl = _mla_qkv(xl.reshape(batch, seq_len, d), mod_lat, ng, win, qn, kvn, wqb,
                                  wkvb, _rope_tables(seq_len), tm=tm)
            qc, kc, vc = _mla_qkv(xc.reshape(batch, ctx_len, d), mod_ctx, ng, win, qn, kvn, wqb,
                                  wkvb, _identity_rope_tables(ctx_len), tm=ctx_len)
            a_lat = _mla_attn(ql, kc, vc, kl, vl, tq=256, kv_chunk=256)
            xl = _proj_ffn(xl, a_lat.reshape(batch * seq_len, -1), mod_lat, ng, wo, wgu, wd, i,
                           tm=tm)
            if ctx_after:
                a_ctx = _ctx_attn(qc, kc, vc)
                xc = _proj_ffn(xc, a_ctx.reshape(batch * ctx_len, -1), mod_ctx, ng, wo, wgu, wd, i,
                               tm=ctx_len)
        else:
            w_in = na_w_in[j].astype(BF16)
            wo = na_w_o[j].astype(BF16)
            qkv = _norm_proj(xl, mod_lat, ng, w_in, tm=tm)
            kvc = _norm_proj(xc, mod_ctx, ng, w_in[:, NA_WIDTH:], tm=ctx_len)
            a_lat = _na_attn(qkv, kvc, _na_bias_table(na_rpb[j]), batch=batch, seq_len=seq_len,
                             ctx_len=ctx_len)
            xl = _proj_ffn(xl, a_lat, mod_lat, ng, wo, wgu, wd, i, tm=tm)
            if ctx_after:
                raise NotImplementedError("context output of a neighbourhood layer")
    return xl.reshape(batch, seq_len, d)
```

```python
import functools

import jax
import jax.numpy as jnp
import numpy as np
from jax import lax
from jax.experimental import pallas as pl
from jax.experimental.pallas import tpu as pltpu

D_MODEL = 1024
DEPTH = 4
GRID_W = 64
N_MIXERS = 3
MIX_POOL, MIX_MLA, MIX_NA = 0, 1, 2
RMS_EPS = 1e-6

POOL_WINDOWS = (2, 4, 8, 16)
POOL_GROUP_DIM = D_MODEL // len(POOL_WINDOWS)
POOL_HALO = 8

MLA_HEADS = 8
MLA_NOPE = 128
MLA_ROPE = 64
MLA_V = 128
MLA_Q_LORA = 384
MLA_KV_LORA = 256
MLA_QK_PAD = 256
MLA_SCALE = (MLA_NOPE + MLA_ROPE) ** -0.5
ROPE_BASE = 10000.0

NA_HEADS = 16
NA_HEAD_DIM = 64
NA_WIDTH = NA_HEADS * NA_HEAD_DIM
NA_ROWS = 8
NA_COLS = 16
NA_SCALE = NA_HEAD_DIM ** -0.5
NA_ROW_BLOCK = 4
NA_WIN_ROWS = NA_ROW_BLOCK + NA_ROWS

FFN_HIDDEN = 2816
FFN_CHUNK = 256
FFN_SUB_ROWS = 512

LANES = 128
VMEM_LIMIT_BYTES = 56 * 1024 * 1024

NEG = -1e30
LOG2E = float(np.log2(np.e))

BF16 = jnp.bfloat16
F32 = jnp.float32


def _params(n_axes, flags=None):
    return pltpu.CompilerParams(
        dimension_semantics=("arbitrary",) * n_axes, vmem_limit_bytes=VMEM_LIMIT_BYTES,
        flags=flags)


def _const_spec(shape):
    nd = len(shape)
    return pl.BlockSpec(shape, lambda *_: (0,) * nd, pipeline_mode=pl.Buffered(1))


def _layer_spec(stacked_shape, layer):
    rest = tuple(stacked_shape[1:])
    return pl.BlockSpec((None,) + rest, lambda *_: (layer,) + (0,) * len(rest),
                        pipeline_mode=pl.Buffered(1))


def _rms(xf, g):
    return xf * lax.rsqrt(jnp.mean(xf * xf, axis=-1, keepdims=True) + RMS_EPS) * g


def _norm_mod(xf, g, shift, scale):
    return _rms(xf, g * (1.0 + scale)) + shift


def _mm(a, b):
    return jnp.dot(a.astype(BF16), b, preferred_element_type=F32)


def _mm_nt(a, b):
    return lax.dot_general(a, b, (((1,), (1,)), ((), ())), preferred_element_type=F32)


def _silu(x):
    return x / (1.0 + jnp.exp(-x))


def _ffn(x1, mod, ng_ref, wgu_ref, wd_ref, abuf, rs):
    sh_f, sc_f, g_f = mod[3:4], mod[4:5], mod[5:6]
    h2 = _norm_mod(x1, ng_ref[2:3], sh_f, sc_f).astype(BF16)
    for c in range(FFN_HIDDEN // FFN_CHUNK):
        lo = c * FFN_CHUNK
        g = jnp.dot(h2, wgu_ref[:, lo:lo + FFN_CHUNK], preferred_element_type=F32)
        u = jnp.dot(h2, wgu_ref[:, FFN_HIDDEN + lo:FFN_HIDDEN + lo + FFN_CHUNK],
                    preferred_element_type=F32)
        abuf[rs, lo:lo + FFN_CHUNK] = (_silu(g) * u).astype(BF16)
    f = jnp.dot(abuf[rs, :], wd_ref[...], preferred_element_type=F32)
    return x1 + _rms(f, ng_ref[3:4] * g_f)


def _sub_tiles(tm):
    sub = min(tm, FFN_SUB_ROWS)
    return [slice(r0, r0 + sub) for r0 in range(0, tm, sub)]


def _adaln_kernel(cond_ref, w_ref, b_ref, o_ref):
    cond = cond_ref[...]
    o_ref[0] = _mm(_silu(cond), w_ref[0].astype(BF16)) + b_ref[0]


def _adaln(cond, ada_w, ada_b):
    rows = cond.shape[0]
    tn = 1024
    n_out = ada_w.shape[-1]
    return pl.pallas_call(
        _adaln_kernel,
        grid=(DEPTH, n_out // tn),
        in_specs=[
            pl.BlockSpec((rows, D_MODEL), lambda l, n: (0, 0)),
            pl.BlockSpec((1, D_MODEL, tn), lambda l, n: (l, 0, n)),
            pl.BlockSpec((1, 1, tn), lambda l, n: (l, 0, n)),
        ],
        out_specs=pl.BlockSpec((1, rows, tn), lambda l, n: (l, 0, n)),
        out_shape=jax.ShapeDtypeStruct((DEPTH, rows, n_out), F32),
        compiler_params=_params(2),
        name="adaln",
    )(cond, ada_w, ada_b.reshape(DEPTH, 1, n_out))


def _pool_ffn_kernel(x_ref, xp_ref, xn_ref, mod_ref, ng_ref, pw_ref, ps_ref, wgu_ref, wd_ref,
                     o_ref, hbuf, abuf, *, tm, seq_len):
    tiles_per_seq = seq_len // tm
    p0 = (pl.program_id(0) % tiles_per_seq) * tm
    mod = mod_ref[0]
    sh_m, sc_m, g_m = mod[0:1], mod[1:2], mod[2:3]

    def hfun(xv):
        return _norm_mod(xv, ng_ref[0:1], sh_m, sc_m)

    hbuf[0:POOL_HALO, :] = jnp.where(p0 > 0, hfun(xp_ref[...]), 0.0)
    hbuf[POOL_HALO:POOL_HALO + tm, :] = hfun(x_ref[...])
    hbuf[POOL_HALO + tm:, :] = jnp.where(p0 + tm < seq_len, hfun(xn_ref[...]), 0.0)

    for rs in _sub_tiles(tm):
        n_rows = rs.stop - rs.start
        base = POOL_HALO + rs.start
        pos = p0 + rs.start + lax.broadcasted_iota(jnp.int32, (n_rows, 1), 0)
        ys = []
        for g, win in enumerate(POOL_WINDOWS):
            cols = slice(g * POOL_GROUP_DIM, (g + 1) * POOL_GROUP_DIM)
            before, after = win // 2, win - win // 2
            s = None
            for d in range(-before, after):
                term = hbuf[base + d:base + d + n_rows, cols]
                s = term if s is None else s + term
            lo = jnp.maximum(pos - before, 0)
            hi = jnp.minimum(pos + after, seq_len)
            inv_cnt = 1.0 / (hi - lo).astype(F32)
            pooled = s * inv_cnt - hbuf[base:base + n_rows, cols]
            ys.append(_mm(pooled, pw_ref[g]))
        y = jnp.concatenate(ys, axis=-1) * ps_ref[...]
        x1 = x_ref[rs, :] + _rms(y, ng_ref[1:2] * g_m)
        o_ref[rs, :] = _ffn(x1, mod, ng_ref, wgu_ref, wd_ref, abuf, rs)


def _pool_ffn(x2d, mod, ng, pw, ps, wgu, wd, layer, *, seq_len, tm):
    t_total = x2d.shape[0]
    n_tiles = t_total // tm
    tiles_per_mod = n_tiles // mod.shape[0]
    halo_blocks = t_total // POOL_HALO
    per_tile = tm // POOL_HALO
    kern = functools.partial(_pool_ffn_kernel, tm=tm, seq_len=seq_len)
    return pl.pallas_call(
        kern,
        grid=(n_tiles,),
        in_specs=[
            pl.BlockSpec((tm, D_MODEL), lambda t: (t, 0)),
            pl.BlockSpec((POOL_HALO, D_MODEL), lambda t: (jnp.maximum(t * per_tile - 1, 0), 0)),
            pl.BlockSpec((POOL_HALO, D_MODEL),
                         lambda t: (jnp.minimum((t + 1) * per_tile, halo_blocks - 1), 0)),
            pl.BlockSpec((1, 6, D_MODEL), lambda t: (t // tiles_per_mod, 0, 0)),
            _const_spec((4, D_MODEL)),
            _const_spec(pw.shape),
            _const_spec((1, D_MODEL)),
            _layer_spec(wgu.shape, layer),
            _layer_spec(wd.shape, layer),
        ],
        out_specs=pl.BlockSpec((tm, D_MODEL), lambda t: (t, 0)),
        out_shape=jax.ShapeDtypeStruct(x2d.shape, F32),
        scratch_shapes=[
            pltpu.VMEM((tm + 2 * POOL_HALO, D_MODEL), F32),
            pltpu.VMEM((tm, FFN_HIDDEN), BF16),
        ],
        compiler_params=_params(1),
        name="pool_ffn",
    )(x2d, x2d, x2d, mod, ng, pw, ps, wgu, wd)


def _proj_ffn_kernel(x_ref, a_ref, mod_ref, ng_ref, wo_ref, wgu_ref, wd_ref, o_ref, abuf):
    mod = mod_ref[0]
    for rs in _sub_tiles(x_ref.shape[0]):
        y = jnp.dot(a_ref[rs, :], wo_ref[...], preferred_element_type=F32)
        x1 = x_ref[rs, :] + _rms(y, ng_ref[1:2] * mod[2:3])
        o_ref[rs, :] = _ffn(x1, mod, ng_ref, wgu_ref, wd_ref, abuf, rs)


def _proj_ffn(x2d, attn2d, mod, ng, wo, wgu, wd, layer, *, tm):
    n_tiles = x2d.shape[0] // tm
    tiles_per_mod = n_tiles // mod.shape[0]
    return pl.pallas_call(
        _proj_ffn_kernel,
        grid=(n_tiles,),
        in_specs=[
            pl.BlockSpec((tm, D_MODEL), lambda t: (t, 0)),
            pl.BlockSpec((tm, attn2d.shape[1]), lambda t: (t, 0)),
            pl.BlockSpec((1, 6, D_MODEL), lambda t: (t // tiles_per_mod, 0, 0)),
            _const_spec((4, D_MODEL)),
            _const_spec(wo.shape),
            _layer_spec(wgu.shape, layer),
            _layer_spec(wd.shape, layer),
        ],
        out_specs=pl.BlockSpec((tm, D_MODEL), lambda t: (t, 0)),
        out_shape=jax.ShapeDtypeStruct(x2d.shape, F32),
        scratch_shapes=[pltpu.VMEM((tm, FFN_HIDDEN), BF16)],
        compiler_params=_params(1),
        name="proj_ffn",
    )(x2d, attn2d, mod, ng, wo, wgu, wd)


def _norm_proj_kernel(x_ref, mod_ref, ng_ref, w_ref, o_ref, *, n_chunks, chunk):
    mod = mod_ref[0]
    h = _norm_mod(x_ref[...], ng_ref[0:1], mod[0:1], mod[1:2]).astype(BF16)
    for c in range(n_chunks):
        sl = slice(c * chunk, (c + 1) * chunk)
        o_ref[:, sl] = jnp.dot(h, w_ref[:, sl], preferred_element_type=F32).astype(BF16)


def _norm_proj(x2d, mod, ng, w, *, tm):
    n_tiles = x2d.shape[0] // tm
    tiles_per_mod = n_tiles // mod.shape[0]
    n_out = w.shape[1]
    chunk = 1024
    kern = functools.partial(_norm_proj_kernel, n_chunks=n_out // chunk, chunk=chunk)
    return pl.pallas_call(
        kern,
        grid=(n_tiles,),
        in_specs=[
            pl.BlockSpec((tm, D_MODEL), lambda t: (t, 0)),
            pl.BlockSpec((1, 6, D_MODEL), lambda t: (t // tiles_per_mod, 0, 0)),
            _const_spec((4, D_MODEL)),
            _const_spec(w.shape),
        ],
        out_specs=pl.BlockSpec((tm, n_out), lambda t: (t, 0)),
        out_shape=jax.ShapeDtypeStruct((x2d.shape[0], n_out), BF16),
        compiler_params=_params(1),
        name="norm_proj",
    )(x2d, mod, ng, w)


def _rope(x, c, sa, sb):
    return x * c + pltpu.roll(x, 32, 1) * sa + pltpu.roll(x, 96, 1) * sb


def _mla_qkv_kernel(x_ref, mod_ref, ng_ref, win_ref, qn_ref, kvn_ref, wqb_ref, wkvb_ref,
                    c_ref, sa_ref, sb_ref, q_ref, k_ref, v_ref):
    mod = mod_ref[0]
    h = _norm_mod(x_ref[0], ng_ref[0:1], mod[0:1], mod[1:2])
    a = _mm(h, win_ref[...])
    q_pre = _mm(_rms(a[:, :MLA_Q_LORA], qn_ref[...]), wqb_ref[...])
    kv = _mm(_rms(a[:, MLA_Q_LORA:MLA_Q_LORA + MLA_KV_LORA], kvn_ref[...]), wkvb_ref[...])
    c, sa, sb = c_ref[...], sa_ref[...], sb_ref[...]
    k_rope = _rope(a[:, MLA_Q_LORA + MLA_KV_LORA:], c, sa, sb)
    v0 = MLA_HEADS * MLA_NOPE
    for hd in range(MLA_HEADS):
        q0 = hd * MLA_QK_PAD
        q_rope = _rope(q_pre[:, q0 + MLA_NOPE:q0 + MLA_QK_PAD], c, sa, sb)
        q_ref[0, hd] = jnp.concatenate(
            [q_pre[:, q0:q0 + MLA_NOPE], q_rope], axis=-1).astype(BF16)
        k_ref[0, hd] = jnp.concatenate(
            [kv[:, hd * MLA_NOPE:(hd + 1) * MLA_NOPE], k_rope], axis=-1).astype(BF16)
        v_ref[0, hd] = kv[:, v0 + hd * MLA_V:v0 + (hd + 1) * MLA_V].astype(BF16)


def _mla_qkv(x3d, mod, ng, win, qn, kvn, wqb, wkvb, tabs, *, tm):
    b, s, _ = x3d.shape
    per_batch = mod.shape[0] == b
    tab_spec = pl.BlockSpec((tm, LANES), lambda i, t: (t, 0))
    qk_shape = jax.ShapeDtypeStruct((b, MLA_HEADS, s, MLA_QK_PAD), BF16)
    return pl.pallas_call(
        _mla_qkv_kernel,
        grid=(b, s // tm),
        in_specs=[
            pl.BlockSpec((1, tm, D_MODEL), lambda i, t: (i, t, 0)),
            pl.BlockSpec((1, 6, D_MODEL), lambda i, t: (i if per_batch else 0, 0, 0)),
            _const_spec((4, D_MODEL)),
            _const_spec(win.shape),
            _const_spec(qn.shape),
            _const_spec(kvn.shape),
            _const_spec(wqb.shape),
            _const_spec(wkvb.shape),
            tab_spec, tab_spec, tab_spec,
        ],
        out_specs=[
            pl.BlockSpec((1, MLA_HEADS, tm, MLA_QK_PAD), lambda i, t: (i, 0, t, 0)),
            pl.BlockSpec((1, MLA_HEADS, tm, MLA_QK_PAD), lambda i, t: (i, 0, t, 0)),
            pl.BlockSpec((1, MLA_HEADS, tm, MLA_V), lambda i, t: (i, 0, t, 0)),
        ],
        out_shape=[qk_shape, qk_shape, jax.ShapeDtypeStruct((b, MLA_HEADS, s, MLA_V), BF16)],
        compiler_params=_params(2),
        name="mla_qkv",
    )(x3d, mod, ng, win, qn, kvn, wqb, wkvb, *tabs)


def _softmax_chunk(q, k, v, scale):
    s = _mm_nt(q, k)
    m = jnp.max(s, axis=-1, keepdims=True)
    p = jnp.exp((s - m) * scale)
    return m, jnp.sum(p, axis=-1, keepdims=True), jnp.dot(
        p.astype(BF16), v, preferred_element_type=F32)


def _lane_tiles(x):
    return [x[:, t * LANES:(t + 1) * LANES] for t in range(x.shape[1] // LANES)]


def _mla_attn_kernel(q_ref, kc_ref, vc_ref, kl_ref, vl_ref, o_ref,
                     vx_sc, s_0, s_1, p_0, p_1, m_0, m_1, *, tq, kv_chunk, pv_chunk):
    s_sc, p_sc, m_sc = (s_0, s_1), (p_0, p_1), (m_0, m_1)
    lc, l = kc_ref.shape[2], kl_ref.shape[2]
    n_tiles = l // tq
    exp2_scale = MLA_SCALE * LOG2E
    k_chunks = [(kc_ref, c0) for c0 in range(0, lc, kv_chunk)]
    k_chunks += [(kl_ref, c0) for c0 in range(0, l, kv_chunk)]
    pv_ends = {lc: 0, **{lc + c0 + pv_chunk: lc + c0 for c0 in range(0, l, pv_chunk)}}

    vx_sc[:lc, :MLA_V] = vc_ref[0, 0]
    vx_sc[lc:, :MLA_V] = vl_ref[0, 0]
    vx_sc[:, MLA_V:] = jnp.ones((lc + l, LANES), BF16)

    def rows(tile):
        start = tile * tq
        return pl.ds(start if isinstance(start, int) else pl.multiple_of(start, tq), tq)

    def stage(t, parity):
        do_values = not isinstance(t, int) or 0 <= t - 1 < n_tiles
        do_scores = not isinstance(t, int) or 0 <= t + 1 < n_tiles
        do_exps = not isinstance(t, int) or 0 <= t < n_tiles
        other = 1 - parity
        if do_scores:
            q = q_ref[0, 0, rows(t + 1), :]
        m_part = acc = None
        for idx, (k_ref, c0) in enumerate(k_chunks):
            cols = slice(idx * kv_chunk, (idx + 1) * kv_chunk)
            if do_scores:
                s = _mm_nt(q, k_ref[0, 0, c0:c0 + kv_chunk, :])
                s_sc[other][:, cols] = s
                for s_tile in _lane_tiles(s):
                    m_part = s_tile if m_part is None else jnp.maximum(m_part, s_tile)
            if do_exps:
                for t0 in range(cols.start, cols.stop, LANES):
                    lanes = slice(t0, t0 + LANES)
                    p_sc[parity][:, lanes] = jnp.exp2(
                        (s_sc[parity][:, lanes] - m_sc[parity][...]) * exp2_scale).astype(BF16)
            if do_values and cols.stop in pv_ends:
                keys = slice(pv_ends[cols.stop], cols.stop)
                pv = jnp.dot(p_sc[other][:, keys], vx_sc[keys, :], preferred_element_type=F32)
                acc = pv if acc is None else acc + pv
        if do_scores:
            m_sc[other][...] = jnp.broadcast_to(
                jnp.max(m_part, axis=-1, keepdims=True), (tq, LANES))
        if do_values:
            o_ref[0, rows(t - 1), :] = (acc[:, :MLA_V] / acc[:, MLA_V:]).astype(BF16)

    stage(-1, 1)
    stage(0, 0)

    def body(i, carry):
        stage(2 * i + 1, 1)
        stage(2 * i + 2, 0)
        return carry

    lax.fori_loop(0, (n_tiles - 2) // 2, body, 0)
    stage(n_tiles - 1, (n_tiles - 1) % 2)
    stage(n_tiles, n_tiles % 2)


def _mla_attn(q, kc, vc, kl, vl, *, tq, kv_chunk):
    b, h, l, _ = q.shape
    lc = kc.shape[2]
    kern = functools.partial(_mla_attn_kernel, tq=tq, kv_chunk=kv_chunk, pv_chunk=4 * kv_chunk)
    return pl.pallas_call(
        kern,
        grid=(b, h),
        in_specs=[
            pl.BlockSpec((1, 1, l, MLA_QK_PAD), lambda i, j: (i, j, 0, 0)),
            pl.BlockSpec((1, 1, lc, MLA_QK_PAD), lambda i, j: (i, j, 0, 0)),
            pl.BlockSpec((1, 1, lc, MLA_V), lambda i, j: (i, j, 0, 0)),
            pl.BlockSpec((1, 1, l, MLA_QK_PAD), lambda i, j: (i, j, 0, 0)),
            pl.BlockSpec((1, 1, l, MLA_V), lambda i, j: (i, j, 0, 0)),
        ],
        out_specs=pl.BlockSpec((1, l, MLA_V), lambda i, j: (i, 0, j)),
        out_shape=jax.ShapeDtypeStruct((b, l, h * MLA_V), BF16),
        scratch_shapes=(
            [pltpu.VMEM((lc + l, 2 * LANES), BF16)] + [pltpu.VMEM((tq, lc + l), F32)] * 2
            + [pltpu.VMEM((tq, lc + l), BF16)] * 2 + [pltpu.VMEM((tq, LANES), F32)] * 2),
        compiler_params=_params(2),
        name="mla_attn",
    )(q, kc, vc, kl, vl)


def _ctx_attn_kernel(q_ref, k_ref, v_ref, o_ref):
    _, l, acc = _softmax_chunk(q_ref[0, 0], k_ref[0, 0], v_ref[0, 0], MLA_SCALE)
    o_ref[0] = (acc / l).astype(BF16)


def _ctx_attn(q, k, v):
    b, h, lc, _ = q.shape
    qk_spec = pl.BlockSpec((1, 1, lc, MLA_QK_PAD), lambda i, j: (i, j, 0, 0))
    return pl.pallas_call(
        _ctx_attn_kernel,
        grid=(b, h),
        in_specs=[qk_spec, qk_spec, pl.BlockSpec((1, 1, lc, MLA_V), lambda i, j: (i, j, 0, 0))],
        out_specs=pl.BlockSpec((1, lc, MLA_V), lambda i, j: (i, 0, j)),
        out_shape=jax.ShapeDtypeStruct((b, lc, h * MLA_V), BF16),
        compiler_params=_params(2),
        name="mla_ctx_attn",
    )(q, k, v)


def _na_attn_kernel(q_ref, k_ref, v_ref, kc_ref, vc_ref, tab_ref, o_ref,
                    vx_sc, vcx_sc, z_0, z_1, m_0, m_1, *, rows):
    z_sc, m_sc = (z_0, z_1), (m_0, m_1)
    tq = NA_ROW_BLOCK * GRID_W
    n_loc = NA_WIN_ROWS * GRID_W
    n_blocks = rows // NA_ROW_BLOCK
    exp2_scale = NA_SCALE * LOG2E
    lane_head = lax.broadcasted_iota(jnp.int32, (1, LANES), 1) // NA_HEAD_DIM
    key_row = lax.broadcasted_iota(jnp.int32, (1, n_loc), 1) // GRID_W

    vx_sc[:, :LANES] = v_ref[...]
    vx_sc[:, LANES:] = jnp.ones(v_ref.shape, BF16)
    vcx_sc[:, :LANES] = vc_ref[...]
    vcx_sc[:, LANES:] = jnp.ones(vc_ref.shape, BF16)

    def window(j):
        win0 = jnp.clip(j * NA_ROW_BLOCK - NA_ROWS // 2, 0, rows - NA_WIN_ROWS)
        return win0, pl.ds(pl.multiple_of(win0 * GRID_W, GRID_W), n_loc)

    def q_rows(j):
        start = j * tq
        return pl.ds(start if isinstance(start, int) else pl.multiple_of(start, tq), tq)

    def scores(j, slot):
        win0, key_rows = window(j)
        q = q_ref[q_rows(j), :]
        zero = jnp.zeros_like(q)
        q2 = jnp.concatenate(
            [jnp.where(lane_head == 0, q, zero), jnp.where(lane_head == 1, q, zero)], axis=0)
        s_loc = _mm_nt(q2, k_ref[key_rows, :])
        z_ctx = _mm_nt(q2, kc_ref[...]) * exp2_scale
        z_sc[slot][:, n_loc:] = z_ctx
        m_ctx = jnp.max(z_ctx, axis=-1, keepdims=True)
        for hh in range(2):
            for i in range(NA_ROW_BLOCK):
                r = j * NA_ROW_BLOCK + i
                k_lo = jnp.clip(r - NA_ROWS // 2, 0, rows - NA_ROWS) - win0
                row_mask = jnp.where((key_row >= k_lo) & (key_row < k_lo + NA_ROWS), 0.0, NEG)
                tiles = []
                for kp in range(NA_WIN_ROWS // 2):
                    dr0 = win0 + 2 * kp - r + NA_ROWS - 1
                    tiles.append(tab_ref[hh, jnp.clip(dr0 + 1, 0, 2 * NA_ROWS - 1)])
                band = slice(hh * tq + i * GRID_W, hh * tq + (i + 1) * GRID_W)
                z = s_loc[band, :] * exp2_scale + (jnp.concatenate(tiles, axis=-1) + row_mask)
                z_sc[slot][band, :n_loc] = z
                m_sc[slot][band, :] = jnp.maximum(
                    jnp.max(z, axis=-1, keepdims=True), m_ctx[band, :])

    def softmax_values(j, slot):
        _, key_rows = window(j)
        p = jnp.exp2(z_sc[slot][...] - m_sc[slot][...]).astype(BF16)
        acc = (jnp.dot(p[:, :n_loc], vx_sc[key_rows, :], preferred_element_type=F32)
               + jnp.dot(p[:, n_loc:], vcx_sc[...], preferred_element_type=F32))
        out = acc[:, :LANES] / acc[:, LANES:]
        o_ref[q_rows(j), :] = jnp.where(lane_head == 0, out[:tq], out[tq:]).astype(BF16)

    scores(0, 0)

    def body(t, carry):
        scores(2 * t + 1, 1)
        softmax_values(2 * t, 0)
        scores(2 * t + 2, 0)
        softmax_values(2 * t + 1, 1)
        return carry

    lax.fori_loop(0, n_blocks // 2 - 1, body, 0)
    scores(n_blocks - 1, 1)
    softmax_values(n_blocks - 2, 0)
    softmax_values(n_blocks - 1, 1)


def _na_attn(qkv, kvc, tab, *, batch, seq_len, ctx_len):
    rows = seq_len // GRID_W
    n_pairs = NA_WIDTH // LANES
    tq = NA_ROW_BLOCK * GRID_W
    n_keys = NA_WIN_ROWS * GRID_W + ctx_len
    kern = functools.partial(_na_attn_kernel, rows=rows)
    return pl.pallas_call(
        kern,
        grid=(n_pairs, batch),
        in_specs=[
            pl.BlockSpec((seq_len, LANES), lambda p, b: (b, p)),
            pl.BlockSpec((seq_len, LANES), lambda p, b: (b, n_pairs + p)),
            pl.BlockSpec((seq_len, LANES), lambda p, b: (b, 2 * n_pairs + p)),
            pl.BlockSpec((ctx_len, LANES), lambda p, b: (b, p)),
            pl.BlockSpec((ctx_len, LANES), lambda p, b: (b, n_pairs + p)),
            pl.BlockSpec((2, 2 * NA_ROWS, GRID_W, LANES), lambda p, b: (p, 0, 0, 0)),
        ],
        out_specs=pl.BlockSpec((seq_len, LANES), lambda p, b: (b, p)),
        out_shape=jax.ShapeDtypeStruct((batch * seq_len, NA_WIDTH), BF16),
        scratch_shapes=(
            [pltpu.VMEM((seq_len, 2 * LANES), BF16), pltpu.VMEM((ctx_len, 2 * LANES), BF16)]
            + [pltpu.VMEM((2 * tq, n_keys), F32)] * 2 + [pltpu.VMEM((2 * tq, 1), F32)] * 2),
        compiler_params=_params(2),
        name="na_attn",
    )(qkv, qkv, qkv, kvc, kvc, tab)


def _na_bias_table(rpb):
    cols = np.arange(GRID_W)
    col_start = np.clip(cols - NA_COLS // 2, 0, GRID_W - NA_COLS)
    col_mask = (cols[None, :] >= col_start[:, None]) & (cols[None, :] < col_start[:, None] + NA_COLS)
    dc_idx = np.clip(cols[None, :] - cols[:, None] + NA_COLS - 1, 0, 2 * NA_COLS - 2)
    t = jnp.where(col_mask[None, None], rpb.astype(F32)[:, :, dc_idx] * LOG2E, NEG)
    zero = jnp.zeros_like(t[:, :1])
    left = jnp.concatenate([zero, t], axis=1)
    right = jnp.concatenate([t, zero], axis=1)
    return jnp.concatenate([left, right], axis=-1)


def _rope_tables(seq_len):
    t = np.arange(seq_len)
    row = (t // GRID_W).astype(np.float32)
    col = (t % GRID_W).astype(np.float32)
    half = MLA_ROPE // 2
    inv = jnp.asarray(ROPE_BASE, F32) ** (-jnp.arange(0, half, 2, dtype=F32) / half)
    ang = jnp.concatenate([row[:, None] * inv, col[:, None] * inv], axis=-1)
    cos, sin = jnp.cos(ang), jnp.sin(ang)
    z32 = jnp.zeros_like(cos)
    z64 = jnp.zeros((seq_len, 64), F32)
    return (jnp.concatenate([cos, cos, z64], axis=-1),
            jnp.concatenate([z32, sin, z64], axis=-1),
            jnp.concatenate([-sin, z32, z64], axis=-1))


def _identity_rope_tables(seq_len):
    ones = jnp.ones((seq_len, 64), F32)
    z = jnp.zeros((seq_len, 64), F32)
    c = jnp.concatenate([ones, z], axis=-1)
    zz = jnp.zeros((seq_len, LANES), F32)
    return c, zz, zz


def kernel(x, c, ctx, c_ctx, ada_w, ada_b, norm_g, ffn_w_gu, ffn_w_down, pool_w, pool_scale,
           mla_w_in, mla_q_norm, mla_kv_norm, mla_w_qb, mla_w_kvb, mla_w_o, na_w_in, na_rpb,
           na_w_o):
    batch, seq_len, d = x.shape
    ctx_len = ctx.shape[1]
    tm = 512

    n_cond = -(-(batch + 1) // 8) * 8
    cond = jnp.zeros((n_cond, d), F32).at[:batch].set(c).at[batch].set(c_ctx)
    mod_all = _adaln(cond, ada_w, ada_b).reshape(DEPTH, n_cond, 6, d)

    wgu = ffn_w_gu.astype(BF16)
    wd = ffn_w_down.astype(BF16)

    xl = x.reshape(batch * seq_len, d)
    xc = ctx.reshape(batch * ctx_len, d)

    for i in range(DEPTH):
        kind = i % N_MIXERS
        j = i // N_MIXERS
        ctx_after = any(l % N_MIXERS != MIX_POOL for l in range(i + 1, DEPTH))
        mod_lat = mod_all[i, :batch]
        mod_ctx = mod_all[i, batch:batch + 1]
        ng = norm_g[i]

        if kind == MIX_POOL:
            pw = pool_w[j].astype(BF16)
            ps = pool_scale[j].reshape(1, d)
            xl = _pool_ffn(xl, mod_lat, ng, pw, ps, wgu, wd, i, seq_len=seq_len, tm=tm)
            if ctx_after:
                xc = _pool_ffn(xc, mod_ctx, ng, pw, ps, wgu, wd, i, seq_len=ctx_len, tm=ctx_len)
        elif kind == MIX_MLA:
            win = jnp.pad(mla_w_in[j], ((0, 0), (0, 64))).astype(BF16)
            wqb = jnp.pad(
                mla_w_qb[j].reshape(MLA_Q_LORA, MLA_HEADS, MLA_NOPE + MLA_ROPE),
                ((0, 0), (0, 0), (0, MLA_QK_PAD - MLA_NOPE - MLA_ROPE)),
            ).reshape(MLA_Q_LORA, MLA_HEADS * MLA_QK_PAD).astype(BF16)
            wkvb = mla_w_kvb[j].reshape(MLA_KV_LORA, MLA_HEADS, 2, MLA_NOPE).transpose(
                0, 2, 1, 3).reshape(MLA_KV_LORA, 2 * MLA_HEADS * MLA_NOPE).astype(BF16)
            qn = mla_q_norm[j].reshape(1, MLA_Q_LORA)
            kvn = mla_kv_norm[j].reshape(1, MLA_KV_LORA)
            wo = mla_w_o[j].astype(BF16)
            ql, kl, vl = _mla_qkv(xl.reshape(batch, seq_len, d), mod_lat, ng, win, qn, kvn, wqb,
                                  wkvb, _rope_tables(seq_len), tm=tm)
            qc, kc, vc = _mla_qkv(xc.reshape(batch, ctx_len, d), mod_ctx, ng, win, qn, kvn, wqb,
                                  wkvb, _identity_rope_tables(ctx_len), tm=ctx_len)
            a_lat = _mla_attn(ql, kc, vc, kl, vl, tq=256, kv_chunk=256)
            xl = _proj_ffn(xl, a_lat.reshape(batch * seq_len, -1), mod_lat, ng, wo, wgu, wd, i,
                           tm=tm)
            if ctx_after:
                a_ctx = _ctx_attn(qc, kc, vc)
                xc = _proj_ffn(xc, a_ctx.reshape(batch * ctx_len, -1), mod_ctx, ng, wo, wgu, wd, i,
                               tm=ctx_len)
        else:
            w_in = na_w_in[j].astype(BF16)
            wo = na_w_o[j].astype(BF16)
            qkv = _norm_proj(xl, mod_lat, ng, w_in, tm=tm)
            kvc = _norm_proj(xc, mod_ctx, ng, w_in[:, NA_WIDTH:], tm=ctx_len)
            a_lat = _na_attn(qkv, kvc, _na_bias_table(na_rpb[j]), batch=batch, seq_len=seq_len,
                             ctx_len=ctx_len)
            xl = _proj_ffn(xl, a_lat, mod_lat, ng, wo, wgu, wd, i, tm=tm)
            if ctx_after:
                raise NotImplementedError("context output of a neighbourhood layer")
    return xl.reshape(batch, seq_len, d)
```

```python
import functools

import jax
import jax.numpy as jnp
import numpy as np
from jax import lax
from jax.experimental import pallas as pl
from jax.experimental.pallas import tpu as pltpu

D_MODEL = 1024
DEPTH = 4
GRID_W = 64
N_MIXERS = 3
MIX_POOL, MIX_MLA, MIX_NA = 0, 1, 2
RMS_EPS = 1e-6

POOL_WINDOWS = (2, 4, 8, 16)
POOL_GROUP_DIM = D_MODEL // len(POOL_WINDOWS)
POOL_HALO = 8

MLA_HEADS = 8
MLA_NOPE = 128
MLA_ROPE = 64
MLA_V = 128
MLA_Q_LORA = 384
MLA_KV_LORA = 256
MLA_QK_PAD = 256
MLA_SCALE = (MLA_NOPE + MLA_ROPE) ** -0.5
ROPE_BASE = 10000.0

NA_HEADS = 16
NA_HEAD_DIM = 64
NA_WIDTH = NA_HEADS * NA_HEAD_DIM
NA_ROWS = 8
NA_COLS = 16
NA_SCALE = NA_HEAD_DIM ** -0.5
NA_ROW_BLOCK = 4
NA_WIN_ROWS = NA_ROW_BLOCK + NA_ROWS

FFN_HIDDEN = 2816
FFN_CHUNK = 256
FFN_NORM_GROUPS = 8

LANES = 128
VMEM_LIMIT_BYTES = 56 * 1024 * 1024

NEG = -1e30
LOG2E = float(np.log2(np.e))

BF16 = jnp.bfloat16
F32 = jnp.float32


def _params(n_axes, flags=None):
    return pltpu.CompilerParams(
        dimension_semantics=("arbitrary",) * n_axes, vmem_limit_bytes=VMEM_LIMIT_BYTES,
        flags=flags)


def _const_spec(shape):
    nd = len(shape)
    return pl.BlockSpec(shape, lambda *_: (0,) * nd, pipeline_mode=pl.Buffered(1))


def _layer_spec(stacked_shape, layer):
    rest = tuple(stacked_shape[1:])
    return pl.BlockSpec((None,) + rest, lambda *_: (layer,) + (0,) * len(rest),
                        pipeline_mode=pl.Buffered(1))


def _rms(xf, g):
    return xf * lax.rsqrt(jnp.mean(xf * xf, axis=-1, keepdims=True) + RMS_EPS) * g


def _norm_mod(xf, g, shift, scale):
    return _rms(xf, g * (1.0 + scale)) + shift


def _mm(a, b):
    return jnp.dot(a.astype(BF16), b, preferred_element_type=F32)


def _mm_nt(a, b):
    return lax.dot_general(a, b, (((1,), (1,)), ((), ())), preferred_element_type=F32)


def _silu(x):
    return x / (1.0 + jnp.exp(-x))


def _adaln_kernel(cond_ref, w_ref, b_ref, o_ref):
    cond = cond_ref[...]
    o_ref[0] = _mm(_silu(cond), w_ref[0].astype(BF16)) + b_ref[0]


def _adaln(cond, ada_w, ada_b):
    rows = cond.shape[0]
    tn = 1024
    n_out = ada_w.shape[-1]
    return pl.pallas_call(
        _adaln_kernel,
        grid=(DEPTH, n_out // tn),
        in_specs=[
            pl.BlockSpec((rows, D_MODEL), lambda l, n: (0, 0)),
            pl.BlockSpec((1, D_MODEL, tn), lambda l, n: (l, 0, n)),
            pl.BlockSpec((1, 1, tn), lambda l, n: (l, 0, n)),
        ],
        out_specs=pl.BlockSpec((1, rows, tn), lambda l, n: (l, 0, n)),
        out_shape=jax.ShapeDtypeStruct((DEPTH, rows, n_out), F32),
        compiler_params=_params(2),
        name="adaln",
    )(cond, ada_w, ada_b.reshape(DEPTH, 1, n_out))


def _pool_ffn_kernel(x_ref, xp_ref, xn_ref, modc_ref, modp_ref, ng_ref, pw_ref, ps_ref, wgu_ref,
                     wd_ref, o_ref, hbuf, x1_0, x1_1, h2_0, h2_1, abuf, *, tm, seq_len, n_tiles):
    tiles_per_seq = seq_len // tm
    tile = jnp.minimum(pl.program_id(0), n_tiles - 1)
    p0 = (tile % tiles_per_seq) * tm

    def front(mod):
        def hfun(xv):
            return _norm_mod(xv, ng_ref[0:1], mod[0:1], mod[1:2])

        hbuf[0:POOL_HALO, :] = jnp.where(p0 > 0, hfun(xp_ref[...]), 0.0)
        hbuf[POOL_HALO:POOL_HALO + tm, :] = hfun(x_ref[...])
        hbuf[POOL_HALO + tm:, :] = jnp.where(p0 + tm < seq_len, hfun(xn_ref[...]), 0.0)

        pos = p0 + lax.broadcasted_iota(jnp.int32, (tm, 1), 0)
        ys = []
        for g, win in enumerate(POOL_WINDOWS):
            cols = slice(g * POOL_GROUP_DIM, (g + 1) * POOL_GROUP_DIM)
            before, after = win // 2, win - win // 2
            s = None
            for d in range(-before, after):
                term = hbuf[POOL_HALO + d:POOL_HALO + d + tm, cols]
                s = term if s is None else s + term
            lo = jnp.maximum(pos - before, 0)
            hi = jnp.minimum(pos + after, seq_len)
            inv_cnt = 1.0 / (hi - lo).astype(F32)
            pooled = s * inv_cnt - hbuf[POOL_HALO:POOL_HALO + tm, cols]
            ys.append(_mm(pooled, pw_ref[g]))
        return jnp.concatenate(ys, axis=-1) * ps_ref[...]

    _staggered_ffn(front, x_ref, modc_ref, modp_ref, ng_ref, wgu_ref, wd_ref, o_ref,
                   (x1_0, x1_1), (h2_0, h2_1), abuf)


def _pool_ffn(x2d, mod, ng, pw, ps, wgu, wd, layer, *, seq_len, tm):
    t_total = x2d.shape[0]
    n_tiles = t_total // tm
    halo_blocks = t_total // POOL_HALO
    per_tile = tm // POOL_HALO
    cur, prev, mod_specs, weight_specs, scratch = _staggered_specs(
        tm, n_tiles, mod, wgu, wd, layer)
    kern = functools.partial(_pool_ffn_kernel, tm=tm, seq_len=seq_len, n_tiles=n_tiles)
    return pl.pallas_call(
        kern,
        grid=(n_tiles + 1,),
        in_specs=[
            pl.BlockSpec((tm, D_MODEL), lambda t: (cur(t), 0)),
            pl.BlockSpec((POOL_HALO, D_MODEL),
                         lambda t: (jnp.maximum(cur(t) * per_tile - 1, 0), 0)),
            pl.BlockSpec((POOL_HALO, D_MODEL),
                         lambda t: (jnp.minimum((cur(t) + 1) * per_tile, halo_blocks - 1), 0)),
            *mod_specs,
            _const_spec((4, D_MODEL)),
            _const_spec(pw.shape),
            _const_spec((1, D_MODEL)),
            *weight_specs,
        ],
        out_specs=pl.BlockSpec((tm, D_MODEL), lambda t: (prev(t), 0)),
        out_shape=jax.ShapeDtypeStruct(x2d.shape, F32),
        scratch_shapes=[pltpu.VMEM((tm + 2 * POOL_HALO, D_MODEL), F32)] + scratch,
        compiler_params=_params(1),
        name="pool_ffn",
    )(x2d, x2d, x2d, mod, mod, ng, pw, ps, wgu, wd)


def _zero_after(v):
    bits = pltpu.bitcast(v, jnp.int32)
    return lax.shift_right_logical(lax.shift_right_logical(bits, 16), 16).astype(F32)


def _ffn_tail(x1_ref, h2_ref, mod, ng_ref, wgu_ref, wd_ref, abuf, anchors):
    for c in range(FFN_HIDDEN // FFN_CHUNK):
        lo = c * FFN_CHUNK
        g = jnp.dot(h2_ref[...], wgu_ref[:, lo:lo + FFN_CHUNK], preferred_element_type=F32)
        u = jnp.dot(h2_ref[...], wgu_ref[:, FFN_HIDDEN + lo:FFN_HIDDEN + lo + FFN_CHUNK],
                    preferred_element_type=F32)
        if 1 <= c <= len(anchors):
            rows = anchors[c - 1].shape[0]
            u = jnp.concatenate([u[:rows] + anchors[c - 1], u[rows:]], axis=0)
        abuf[:, lo:lo + FFN_CHUNK] = (_silu(g) * u).astype(BF16)
    f = jnp.dot(abuf[...], wd_ref[...], preferred_element_type=F32)
    return x1_ref[...] + _rms(f, ng_ref[3:4] * mod[5:6])


def _staggered_ffn(front, x_ref, modc_ref, modp_ref, ng_ref, wgu_ref, wd_ref, o_ref,
                   x1_sc, h2_sc, abuf):
    t = pl.program_id(0)
    tm = x_ref.shape[0]
    group = tm // FFN_NORM_GROUPS

    @pl.when(t == 0)
    def _():
        x1_sc[1][...] = jnp.zeros(x1_sc[1].shape, F32)
        h2_sc[1][...] = jnp.zeros(h2_sc[1].shape, BF16)

    def step(cur, prev):
        mod = modc_ref[0]
        y = front(mod)
        anchors = []
        for r0 in range(0, tm, group):
            rg = slice(r0, r0 + group)
            x1 = x_ref[rg, :] + _rms(y[rg], ng_ref[1:2] * mod[2:3])
            x1_sc[cur][rg, :] = x1
            h2 = _norm_mod(x1, ng_ref[2:3], mod[3:4], mod[4:5])
            h2_sc[cur][rg, :] = h2.astype(BF16)
            anchors.append(_zero_after(h2[:8, :FFN_CHUNK]))
        o_ref[...] = _ffn_tail(x1_sc[prev], h2_sc[prev], modp_ref[0], ng_ref, wgu_ref, wd_ref, abuf,
                               anchors)

    @pl.when(t % 2 == 0)
    def _():
        step(0, 1)

    @pl.when(t % 2 == 1)
    def _():
        step(1, 0)


def _staggered_specs(tm, n_tiles, mod, wgu, wd, layer):
    tiles_per_mod = n_tiles // mod.shape[0]

    def cur(t):
        return jnp.minimum(t, n_tiles - 1)

    def prev(t):
        return jnp.maximum(t - 1, 0)

    mod_specs = [
        pl.BlockSpec((1, 6, D_MODEL), lambda t: (cur(t) // tiles_per_mod, 0, 0)),
        pl.BlockSpec((1, 6, D_MODEL), lambda t: (prev(t) // tiles_per_mod, 0, 0)),
    ]
    weight_specs = [_layer_spec(wgu.shape, layer), _layer_spec(wd.shape, layer)]
    scratch = ([pltpu.VMEM((tm, D_MODEL), F32)] * 2 + [pltpu.VMEM((tm, D_MODEL), BF16)] * 2
               + [pltpu.VMEM((tm, FFN_HIDDEN), BF16)])
    return cur, prev, mod_specs, weight_specs, scratch


def _proj_ffn_kernel(x_ref, a_ref, modc_ref, modp_ref, ng_ref, wo_ref, wgu_ref, wd_ref, o_ref,
                     x1_0, x1_1, h2_0, h2_1, abuf):
    def front(mod):
        del mod
        return jnp.dot(a_ref[...], wo_ref[...], preferred_element_type=F32)

    _staggered_ffn(front, x_ref, modc_ref, modp_ref, ng_ref, wgu_ref, wd_ref, o_ref,
                   (x1_0, x1_1), (h2_0, h2_1), abuf)


def _proj_ffn(x2d, attn2d, mod, ng, wo, wgu, wd, layer, *, tm):
    n_tiles = x2d.shape[0] // tm
    cur, prev, mod_specs, weight_specs, scratch = _staggered_specs(
        tm, n_tiles, mod, wgu, wd, layer)
    return pl.pallas_call(
        _proj_ffn_kernel,
        grid=(n_tiles + 1,),
        in_specs=[
            pl.BlockSpec((tm, D_MODEL), lambda t: (cur(t), 0)),
            pl.BlockSpec((tm, attn2d.shape[1]), lambda t: (cur(t), 0)),
            *mod_specs,
            _const_spec((4, D_MODEL)),
            _const_spec(wo.shape),
            *weight_specs,
        ],
        out_specs=pl.BlockSpec((tm, D_MODEL), lambda t: (prev(t), 0)),
        out_shape=jax.ShapeDtypeStruct(x2d.shape, F32),
        scratch_shapes=scratch,
        compiler_params=_params(1),
        name="proj_ffn",
    )(x2d, attn2d, mod, mod, ng, wo, wgu, wd)


def _norm_proj_kernel(x_ref, mod_ref, ng_ref, w_ref, o_ref, *, n_chunks, chunk):
    mod = mod_ref[0]
    h = _norm_mod(x_ref[...], ng_ref[0:1], mod[0:1], mod[1:2]).astype(BF16)
    for c in range(n_chunks):
        sl = slice(c * chunk, (c + 1) * chunk)
        o_ref[:, sl] = jnp.dot(h, w_ref[:, sl], preferred_element_type=F32).astype(BF16)


def _norm_proj(x2d, mod, ng, w, *, tm):
    n_tiles = x2d.shape[0] // tm
    tiles_per_mod = n_tiles // mod.shape[0]
    n_out = w.shape[1]
    chunk = 1024
    kern = functools.partial(_norm_proj_kernel, n_chunks=n_out // chunk, chunk=chunk)
    return pl.pallas_call(
        kern,
        grid=(n_tiles,),
        in_specs=[
            pl.BlockSpec((tm, D_MODEL), lambda t: (t, 0)),
            pl.BlockSpec((1, 6, D_MODEL), lambda t: (t // tiles_per_mod, 0, 0)),
            _const_spec((4, D_MODEL)),
            _const_spec(w.shape),
        ],
        out_specs=pl.BlockSpec((tm, n_out), lambda t: (t, 0)),
        out_shape=jax.ShapeDtypeStruct((x2d.shape[0], n_out), BF16),
        compiler_params=_params(1),
        name="norm_proj",
    )(x2d, mod, ng, w)


def _rope(x, c, sa, sb):
    return x * c + pltpu.roll(x, 32, 1) * sa + pltpu.roll(x, 96, 1) * sb


def _mla_qkv_kernel(x_ref, mod_ref, ng_ref, win_ref, qn_ref, kvn_ref, wqb_ref, wkvb_ref,
                    c_ref, sa_ref, sb_ref, q_ref, k_ref, v_ref):
    mod = mod_ref[0]
    h = _norm_mod(x_ref[0], ng_ref[0:1], mod[0:1], mod[1:2])
    a = _mm(h, win_ref[...])
    q_pre = _mm(_rms(a[:, :MLA_Q_LORA], qn_ref[...]), wqb_ref[...])
    kv = _mm(_rms(a[:, MLA_Q_LORA:MLA_Q_LORA + MLA_KV_LORA], kvn_ref[...]), wkvb_ref[...])
    c, sa, sb = c_ref[...], sa_ref[...], sb_ref[...]
    k_rope = _rope(a[:, MLA_Q_LORA + MLA_KV_LORA:], c, sa, sb)
    v0 = MLA_HEADS * MLA_NOPE
    for hd in range(MLA_HEADS):
        q0 = hd * MLA_QK_PAD
        q_rope = _rope(q_pre[:, q0 + MLA_NOPE:q0 + MLA_QK_PAD], c, sa, sb)
        q_ref[0, hd] = jnp.concatenate(
            [q_pre[:, q0:q0 + MLA_NOPE], q_rope], axis=-1).astype(BF16)
        k_ref[0, hd] = jnp.concatenate(
            [kv[:, hd * MLA_NOPE:(hd + 1) * MLA_NOPE], k_rope], axis=-1).astype(BF16)
        v_ref[0, hd] = kv[:, v0 + hd * MLA_V:v0 + (hd + 1) * MLA_V].astype(BF16)


def _mla_qkv(x3d, mod, ng, win, qn, kvn, wqb, wkvb, tabs, *, tm):
    b, s, _ = x3d.shape
    per_batch = mod.shape[0] == b
    tab_spec = pl.BlockSpec((tm, LANES), lambda i, t: (t, 0))
    qk_shape = jax.ShapeDtypeStruct((b, MLA_HEADS, s, MLA_QK_PAD), BF16)
    return pl.pallas_call(
        _mla_qkv_kernel,
        grid=(b, s // tm),
        in_specs=[
            pl.BlockSpec((1, tm, D_MODEL), lambda i, t: (i, t, 0)),
            pl.BlockSpec((1, 6, D_MODEL), lambda i, t: (i if per_batch else 0, 0, 0)),
            _const_spec((4, D_MODEL)),
            _const_spec(win.shape),
            _const_spec(qn.shape),
            _const_spec(kvn.shape),
            _const_spec(wqb.shape),
            _const_spec(wkvb.shape),
            tab_spec, tab_spec, tab_spec,
        ],
        out_specs=[
            pl.BlockSpec((1, MLA_HEADS, tm, MLA_QK_PAD), lambda i, t: (i, 0, t, 0)),
            pl.BlockSpec((1, MLA_HEADS, tm, MLA_QK_PAD), lambda i, t: (i, 0, t, 0)),
            pl.BlockSpec((1, MLA_HEADS, tm, MLA_V), lambda i, t: (i, 0, t, 0)),
        ],
        out_shape=[qk_shape, qk_shape, jax.ShapeDtypeStruct((b, MLA_HEADS, s, MLA_V), BF16)],
        compiler_params=_params(2),
        name="mla_qkv",
    )(x3d, mod, ng, win, qn, kvn, wqb, wkvb, *tabs)


def _softmax_chunk(q, k, v, scale):
    s = _mm_nt(q, k)
    m = jnp.max(s, axis=-1, keepdims=True)
    p = jnp.exp((s - m) * scale)
    return m, jnp.sum(p, axis=-1, keepdims=True), jnp.dot(
        p.astype(BF16), v, preferred_element_type=F32)


def _lane_tiles(x):
    return [x[:, t * LANES:(t + 1) * LANES] for t in range(x.shape[1] // LANES)]


def _mla_attn_kernel(q_ref, kc_ref, vc_ref, kl_ref, vl_ref, o_ref,
                     vx_sc, s_0, s_1, p_0, p_1, m_0, m_1, *, tq, kv_chunk, pv_chunk):
    s_sc, p_sc, m_sc = (s_0, s_1), (p_0, p_1), (m_0, m_1)
    lc, l = kc_ref.shape[2], kl_ref.shape[2]
    n_tiles = l // tq
    exp2_scale = MLA_SCALE * LOG2E
    k_chunks = [(kc_ref, c0) for c0 in range(0, lc, kv_chunk)]
    k_chunks += [(kl_ref, c0) for c0 in range(0, l, kv_chunk)]
    pv_ends = {lc: 0, **{lc + c0 + pv_chunk: lc + c0 for c0 in range(0, l, pv_chunk)}}

    vx_sc[:lc, :MLA_V] = vc_ref[0, 0]
    vx_sc[lc:, :MLA_V] = vl_ref[0, 0]
    vx_sc[:, MLA_V:] = jnp.ones((lc + l, LANES), BF16)

    def rows(tile):
        start = tile * tq
        return pl.ds(start if isinstance(start, int) else pl.multiple_of(start, tq), tq)

    def stage(t, parity):
        do_values = not isinstance(t, int) or 0 <= t - 1 < n_tiles
        do_scores = not isinstance(t, int) or 0 <= t + 1 < n_tiles
        do_exps = not isinstance(t, int) or 0 <= t < n_tiles
        other = 1 - parity
        if do_scores:
            q = q_ref[0, 0, rows(t + 1), :]
        m_part = acc = None
        for idx, (k_ref, c0) in enumerate(k_chunks):
            cols = slice(idx * kv_chunk, (idx + 1) * kv_chunk)
            if do_scores:
                s = _mm_nt(q, k_ref[0, 0, c0:c0 + kv_chunk, :])
                s_sc[other][:, cols] = s
                for s_tile in _lane_tiles(s):
                    m_part = s_tile if m_part is None else jnp.maximum(m_part, s_tile)
            if do_exps:
                for t0 in range(cols.start, cols.stop, LANES):
                    lanes = slice(t0, t0 + LANES)
                    p_sc[parity][:, lanes] = jnp.exp2(
                        (s_sc[parity][:, lanes] - m_sc[parity][...]) * exp2_scale).astype(BF16)
            if do_values and cols.stop in pv_ends:
                keys = slice(pv_ends[cols.stop], cols.stop)
                pv = jnp.dot(p_sc[other][:, keys], vx_sc[keys, :], preferred_element_type=F32)
                acc = pv if acc is None else acc + pv
        if do_scores:
            m_sc[other][...] = jnp.broadcast_to(
                jnp.max(m_part, axis=-1, keepdims=True), (tq, LANES))
        if do_values:
            o_ref[0, rows(t - 1), :] = (acc[:, :MLA_V] / acc[:, MLA_V:]).astype(BF16)

    stage(-1, 1)
    stage(0, 0)

    def body(i, carry):
        stage(2 * i + 1, 1)
        stage(2 * i + 2, 0)
        return carry

    lax.fori_loop(0, (n_tiles - 2) // 2, body, 0)
    stage(n_tiles - 1, (n_tiles - 1) % 2)
    stage(n_tiles, n_tiles % 2)


def _mla_attn(q, kc, vc, kl, vl, *, tq, kv_chunk):
    b, h, l, _ = q.shape
    lc = kc.shape[2]
    kern = functools.partial(_mla_attn_kernel, tq=tq, kv_chunk=kv_chunk, pv_chunk=4 * kv_chunk)
    return pl.pallas_call(
        kern,
        grid=(b, h),
        in_specs=[
            pl.BlockSpec((1, 1, l, MLA_QK_PAD), lambda i, j: (i, j, 0, 0)),
            pl.BlockSpec((1, 1, lc, MLA_QK_PAD), lambda i, j: (i, j, 0, 0)),
            pl.BlockSpec((1, 1, lc, MLA_V), lambda i, j: (i, j, 0, 0)),
            pl.BlockSpec((1, 1, l, MLA_QK_PAD), lambda i, j: (i, j, 0, 0)),
            pl.BlockSpec((1, 1, l, MLA_V), lambda i, j: (i, j, 0, 0)),
        ],
        out_specs=pl.BlockSpec((1, l, MLA_V), lambda i, j: (i, 0, j)),
        out_shape=jax.ShapeDtypeStruct((b, l, h * MLA_V), BF16),
        scratch_shapes=(
            [pltpu.VMEM((lc + l, 2 * LANES), BF16)] + [pltpu.VMEM((tq, lc + l), F32)] * 2
            + [pltpu.VMEM((tq, lc + l), BF16)] * 2 + [pltpu.VMEM((tq, LANES), F32)] * 2),
        compiler_params=_params(2),
        name="mla_attn",
    )(q, kc, vc, kl, vl)


def _ctx_attn_kernel(q_ref, k_ref, v_ref, o_ref):
    _, l, acc = _softmax_chunk(q_ref[0, 0], k_ref[0, 0], v_ref[0, 0], MLA_SCALE)
    o_ref[0] = (acc / l).astype(BF16)


def _ctx_attn(q, k, v):
    b, h, lc, _ = q.shape
    qk_spec = pl.BlockSpec((1, 1, lc, MLA_QK_PAD), lambda i, j: (i, j, 0, 0))
    return pl.pallas_call(
        _ctx_attn_kernel,
        grid=(b, h),
        in_specs=[qk_spec, qk_spec, pl.BlockSpec((1, 1, lc, MLA_V), lambda i, j: (i, j, 0, 0))],
        out_specs=pl.BlockSpec((1, lc, MLA_V), lambda i, j: (i, 0, j)),
        out_shape=jax.ShapeDtypeStruct((b, lc, h * MLA_V), BF16),
        compiler_params=_params(2),
        name="mla_ctx_attn",
    )(q, k, v)


def _na_attn_kernel(q_ref, k_ref, v_ref, kc_ref, vc_ref, tab_ref, o_ref,
                    vx_sc, vcx_sc, z_0, z_1, m_0, m_1, *, rows):
    z_sc, m_sc = (z_0, z_1), (m_0, m_1)
    tq = NA_ROW_BLOCK * GRID_W
    n_loc = NA_WIN_ROWS * GRID_W
    n_blocks = rows // NA_ROW_BLOCK
    exp2_scale = NA_SCALE * LOG2E
    lane_head = lax.broadcasted_iota(jnp.int32, (1, LANES), 1) // NA_HEAD_DIM
    key_row = lax.broadcasted_iota(jnp.int32, (1, n_loc), 1) // GRID_W

    vx_sc[:, :LANES] = v_ref[...]
    vx_sc[:, LANES:] = jnp.ones(v_ref.shape, BF16)
    vcx_sc[:, :LANES] = vc_ref[...]
    vcx_sc[:, LANES:] = jnp.ones(vc_ref.shape, BF16)

    def window(j):
        win0 = jnp.clip(j * NA_ROW_BLOCK - NA_ROWS // 2, 0, rows - NA_WIN_ROWS)
        return win0, pl.ds(pl.multiple_of(win0 * GRID_W, GRID_W), n_loc)

    def q_rows(j):
        start = j * tq
        return pl.ds(start if isinstance(start, int) else pl.multiple_of(start, tq), tq)

    def scores(j, slot):
        win0, key_rows = window(j)
        q = q_ref[q_rows(j), :]
        zero = jnp.zeros_like(q)
        q2 = jnp.concatenate(
            [jnp.where(lane_head == 0, q, zero), jnp.where(lane_head == 1, q, zero)], axis=0)
        s_loc = _mm_nt(q2, k_ref[key_rows, :])
        z_ctx = _mm_nt(q2, kc_ref[...]) * exp2_scale
        z_sc[slot][:, n_loc:] = z_ctx
        m_ctx = jnp.max(z_ctx, axis=-1, keepdims=True)
        for hh in range(2):
            for i in range(NA_ROW_BLOCK):
                r = j * NA_ROW_BLOCK + i
                k_lo = jnp.clip(r - NA_ROWS // 2, 0, rows - NA_ROWS) - win0
                row_mask = jnp.where((key_row >= k_lo) & (key_row < k_lo + NA_ROWS), 0.0, NEG)
                tiles = []
                for kp in range(NA_WIN_ROWS // 2):
                    dr0 = win0 + 2 * kp - r + NA_ROWS - 1
                    tiles.append(tab_ref[hh, jnp.clip(dr0 + 1, 0, 2 * NA_ROWS - 1)])
                band = slice(hh * tq + i * GRID_W, hh * tq + (i + 1) * GRID_W)
                z = s_loc[band, :] * exp2_scale + (jnp.concatenate(tiles, axis=-1) + row_mask)
                z_sc[slot][band, :n_loc] = z
                m_sc[slot][band, :] = jnp.maximum(
                    jnp.max(z, axis=-1, keepdims=True), m_ctx[band, :])

    def softmax_values(j, slot):
        _, key_rows = window(j)
        p = jnp.exp2(z_sc[slot][...] - m_sc[slot][...]).astype(BF16)
        acc = (jnp.dot(p[:, :n_loc], vx_sc[key_rows, :], preferred_element_type=F32)
               + jnp.dot(p[:, n_loc:], vcx_sc[...], preferred_element_type=F32))
        out = acc[:, :LANES] / acc[:, LANES:]
        o_ref[q_rows(j), :] = jnp.where(lane_head == 0, out[:tq], out[tq:]).astype(BF16)

    scores(0, 0)

    def body(t, carry):
        scores(2 * t + 1, 1)
        softmax_values(2 * t, 0)
        scores(2 * t + 2, 0)
        softmax_values(2 * t + 1, 1)
        return carry

    lax.fori_loop(0, n_blocks // 2 - 1, body, 0)
    scores(n_blocks - 1, 1)
    softmax_values(n_blocks - 2, 0)
    softmax_values(n_blocks - 1, 1)


def _na_attn(qkv, kvc, tab, *, batch, seq_len, ctx_len):
    rows = seq_len // GRID_W
    n_pairs = NA_WIDTH // LANES
    tq = NA_ROW_BLOCK * GRID_W
    n_keys = NA_WIN_ROWS * GRID_W + ctx_len
    kern = functools.partial(_na_attn_kernel, rows=rows)
    return pl.pallas_call(
        kern,
        grid=(n_pairs, batch),
        in_specs=[
            pl.BlockSpec((seq_len, LANES), lambda p, b: (b, p)),
            pl.BlockSpec((seq_len, LANES), lambda p, b: (b, n_pairs + p)),
            pl.BlockSpec((seq_len, LANES), lambda p, b: (b, 2 * n_pairs + p)),
            pl.BlockSpec((ctx_len, LANES), lambda p, b: (b, p)),
            pl.BlockSpec((ctx_len, LANES), lambda p, b: (b, n_pairs + p)),
            pl.BlockSpec((2, 2 * NA_ROWS, GRID_W, LANES), lambda p, b: (p, 0, 0, 0)),
        ],
        out_specs=pl.BlockSpec((seq_len, LANES), lambda p, b: (b, p)),
        out_shape=jax.ShapeDtypeStruct((batch * seq_len, NA_WIDTH), BF16),
        scratch_shapes=(
            [pltpu.VMEM((seq_len, 2 * LANES), BF16), pltpu.VMEM((ctx_len, 2 * LANES), BF16)]
            + [pltpu.VMEM((2 * tq, n_keys), F32)] * 2 + [pltpu.VMEM((2 * tq, 1), F32)] * 2),
        compiler_params=_params(2),
        name="na_attn",
    )(qkv, qkv, qkv, kvc, kvc, tab)


def _na_bias_table(rpb):
    cols = np.arange(GRID_W)
    col_start = np.clip(cols - NA_COLS // 2, 0, GRID_W - NA_COLS)
    col_mask = (cols[None, :] >= col_start[:, None]) & (cols[None, :] < col_start[:, None] + NA_COLS)
    dc_idx = np.clip(cols[None, :] - cols[:, None] + NA_COLS - 1, 0, 2 * NA_COLS - 2)
    t = jnp.where(col_mask[None, None], rpb.astype(F32)[:, :, dc_idx] * LOG2E, NEG)
    zero = jnp.zeros_like(t[:, :1])
    left = jnp.concatenate([zero, t], axis=1)
    right = jnp.concatenate([t, zero], axis=1)
    return jnp.concatenate([left, right], axis=-1)


def _rope_tables(seq_len):
    t = np.arange(seq_len)
    row = (t // GRID_W).astype(np.float32)
    col = (t % GRID_W).astype(np.float32)
    half = MLA_ROPE // 2
    inv = jnp.asarray(ROPE_BASE, F32) ** (-jnp.arange(0, half, 2, dtype=F32) / half)
    ang = jnp.concatenate([row[:, None] * inv, col[:, None] * inv], axis=-1)
    cos, sin = jnp.cos(ang), jnp.sin(ang)
    z32 = jnp.zeros_like(cos)
    z64 = jnp.zeros((seq_len, 64), F32)
    return (jnp.concatenate([cos, cos, z64], axis=-1),
            jnp.concatenate([z32, sin, z64], axis=-1),
            jnp.concatenate([-sin, z32, z64], axis=-1))


def _identity_rope_tables(seq_len):
    ones = jnp.ones((seq_len, 64), F32)
    z = jnp.zeros((seq_len, 64), F32)
    c = jnp.concatenate([ones, z], axis=-1)
    zz = jnp.zeros((seq_len, LANES), F32)
    return c, zz, zz


def kernel(x, c, ctx, c_ctx, ada_w, ada_b, norm_g, ffn_w_gu, ffn_w_down, pool_w, pool_scale,
           mla_w_in, mla_q_norm, mla_kv_norm, mla_w_qb, mla_w_kvb, mla_w_o, na_w_in, na_rpb,
           na_w_o):
    batch, seq_len, d = x.shape
    ctx_len = ctx.shape[1]
    tm = 512

    n_cond = -(-(batch + 1) // 8) * 8
    cond = jnp.zeros((n_cond, d), F32).at[:batch].set(c).at[batch].set(c_ctx)
    mod_all = _adaln(cond, ada_w, ada_b).reshape(DEPTH, n_cond, 6, d)

    wgu = ffn_w_gu.astype(BF16)
    wd = ffn_w_down.astype(BF16)

    xl = x.reshape(batch * seq_len, d)
    xc = ctx.reshape(batch * ctx_len, d)

    for i in range(DEPTH):
        kind = i % N_MIXERS
        j = i // N_MIXERS
        ctx_after = any(l % N_MIXERS != MIX_POOL for l in range(i + 1, DEPTH))
        mod_lat = mod_all[i, :batch]
        mod_ctx = mod_all[i, batch:batch + 1]
        ng = norm_g[i]

        if kind == MIX_POOL:
            pw = pool_w[j].astype(BF16)
            ps = pool_scale[j].reshape(1, d)
            xl = _pool_ffn(xl, mod_lat, ng, pw, ps, wgu, wd, i, seq_len=seq_len, tm=tm)
            if ctx_after:
                xc = _pool_ffn(xc, mod_ctx, ng, pw, ps, wgu, wd, i, seq_len=ctx_len, tm=ctx_len)
        elif kind == MIX_MLA:
            win = jnp.pad(mla_w_in[j], ((0, 0), (0, 64))).astype(BF16)
            wqb = jnp.pad(
                mla_w_qb[j].reshape(MLA_Q_LORA, MLA_HEADS, MLA_NOPE + MLA_ROPE),
                ((0, 0), (0, 0), (0, MLA_QK_PAD - MLA_NOPE - MLA_ROPE)),
            ).reshape(MLA_Q_LORA, MLA_HEADS * MLA_QK_PAD).astype(BF16)
            wkvb = mla_w_kvb[j].reshape(MLA_KV_LORA, MLA_HEADS, 2, MLA_NOPE).transpose(
                0, 2, 1, 3).reshape(MLA_KV_LORA, 2 * MLA_HEADS * MLA_NOPE).astype(BF16)
            qn = mla_q_norm[j].reshape(1, MLA_Q_LORA)
            kvn = mla_kv_norm[j].reshape(1, MLA_KV_LORA)
            wo = mla_w_o[j].astype(BF16)
            ql, kl, vl = _mla_qkv(xl.reshape(batch, seq_len, d), mod_lat, ng, win, qn, kvn, wqb,
                                  wkvb, _rope_tables(seq_len), tm=tm)
            qc, kc, vc = _mla_qkv(xc.reshape(batch, ctx_len, d), mod_ctx, ng, win, qn, kvn, wqb,
                                  wkvb, _identity_rope_tables(ctx_len), tm=ctx_len)
            a_lat = _mla_attn(ql, kc, vc, kl, vl, tq=256, kv_chunk=256)
            xl = _proj_ffn(xl, a_lat.reshape(batch * seq_len, -1), mod_lat, ng, wo, wgu, wd, i,
                           tm=tm)
            if ctx_after:
                a_ctx = _ctx_attn(qc, kc, vc)
                xc = _proj_ffn(xc, a_ctx.reshape(batch * ctx_len, -1), mod_ctx, ng, wo, wgu, wd, i,
                               tm=ctx_len)
        else:
            w_in = na_w_in[j].astype(BF16)
            wo = na_w_o[j].astype(BF16)
            qkv = _norm_proj(xl, mod_lat, ng, w_in, tm=tm)
            kvc = _norm_proj(xc, mod_ctx, ng, w_in[:, NA_WIDTH:], tm=ctx_len)
            a_lat = _na_attn(qkv, kvc, _na_bias_table(na_rpb[j]), batch=batch, seq_len=seq_len,
                             ctx_len=ctx_len)
            xl = _proj_ffn(xl, a_lat, mod_lat, ng, wo, wgu, wd, i, tm=tm)
            if ctx_after:
                raise NotImplementedError("context output of a neighbourhood layer")
    return xl.reshape(batch, seq_len, d)
```

```python
import functools

import jax
import jax.numpy as jnp
import numpy as np
from jax import lax
from jax.experimental import pallas as pl
from jax.experimental.pallas import tpu as pltpu

D_MODEL = 1024
DEPTH = 4
GRID_W = 64
N_MIXERS = 3
MIX_POOL, MIX_MLA, MIX_NA = 0, 1, 2
RMS_EPS = 1e-6

POOL_WINDOWS = (2, 4, 8, 16)
POOL_GROUP_DIM = D_MODEL // len(POOL_WINDOWS)
POOL_HALO = 8

MLA_HEADS = 8
MLA_NOPE = 128
MLA_ROPE = 64
MLA_V = 128
MLA_Q_LORA = 384
MLA_KV_LORA = 256
MLA_QK_PAD = 256
MLA_SCALE = (MLA_NOPE + MLA_ROPE) ** -0.5
ROPE_BASE = 10000.0

NA_HEADS = 16
NA_HEAD_DIM = 64
NA_WIDTH = NA_HEADS * NA_HEAD_DIM
NA_ROWS = 8
NA_COLS = 16
NA_SCALE = NA_HEAD_DIM ** -0.5
NA_ROW_BLOCK = 4
NA_WIN_ROWS = NA_ROW_BLOCK + NA_ROWS

FFN_HIDDEN = 2816
FFN_CHUNK = 256
FFN_NORM_GROUPS = 8

LANES = 128
VMEM_LIMIT_BYTES = 56 * 1024 * 1024

NEG = -1e30
LOG2E = float(np.log2(np.e))

BF16 = jnp.bfloat16
F32 = jnp.float32


def _params(n_axes, flags=None):
    return pltpu.CompilerParams(
        dimension_semantics=("arbitrary",) * n_axes, vmem_limit_bytes=VMEM_LIMIT_BYTES,
        flags=flags)


def _const_spec(shape):
    nd = len(shape)
    return pl.BlockSpec(shape, lambda *_: (0,) * nd, pipeline_mode=pl.Buffered(1))


def _layer_spec(stacked_shape, layer):
    rest = tuple(stacked_shape[1:])
    return pl.BlockSpec((None,) + rest, lambda *_: (layer,) + (0,) * len(rest),
                        pipeline_mode=pl.Buffered(1))


def _rms(xf, g):
    return xf * lax.rsqrt(jnp.mean(xf * xf, axis=-1, keepdims=True) + RMS_EPS) * g


def _norm_mod(xf, g, shift, scale):
    return _rms(xf, g * (1.0 + scale)) + shift


def _mm(a, b):
    return jnp.dot(a.astype(BF16), b, preferred_element_type=F32)


def _mm_nt(a, b):
    return lax.dot_general(a, b, (((1,), (1,)), ((), ())), preferred_element_type=F32)


def _silu(x):
    return x / (1.0 + jnp.exp(-x))


def _adaln_kernel(cond_ref, w_ref, b_ref, o_ref):
    cond = cond_ref[...]
    o_ref[0] = _mm(_silu(cond), w_ref[0].astype(BF16)) + b_ref[0]


def _adaln(cond, ada_w, ada_b):
    rows = cond.shape[0]
    tn = 1024
    n_out = ada_w.shape[-1]
    return pl.pallas_call(
        _adaln_kernel,
        grid=(DEPTH, n_out // tn),
        in_specs=[
            pl.BlockSpec((rows, D_MODEL), lambda l, n: (0, 0)),
            pl.BlockSpec((1, D_MODEL, tn), lambda l, n: (l, 0, n)),
            pl.BlockSpec((1, 1, tn), lambda l, n: (l, 0, n)),
        ],
        out_specs=pl.BlockSpec((1, rows, tn), lambda l, n: (l, 0, n)),
        out_shape=jax.ShapeDtypeStruct((DEPTH, rows, n_out), F32),
        compiler_params=_params(2),
        name="adaln",
    )(cond, ada_w, ada_b.reshape(DEPTH, 1, n_out))


def _pool_ffn_kernel(x_ref, xp_ref, xn_ref, modc_ref, modp_ref, ng_ref, pw_ref, ps_ref, wgu_ref,
                     wd_ref, o_ref, hbuf, x1_0, x1_1, h2_0, h2_1, abuf, *, tm, seq_len, n_tiles):
    tiles_per_seq = seq_len // tm
    tile = jnp.minimum(pl.program_id(0), n_tiles - 1)
    p0 = (tile % tiles_per_seq) * tm

    def front(mod):
        def hfun(xv):
            return _norm_mod(xv, ng_ref[0:1], mod[0:1], mod[1:2])

        hbuf[0:POOL_HALO, :] = jnp.where(p0 > 0, hfun(xp_ref[...]), 0.0)
        hbuf[POOL_HALO:POOL_HALO + tm, :] = hfun(x_ref[...])
        hbuf[POOL_HALO + tm:, :] = jnp.where(p0 + tm < seq_len, hfun(xn_ref[...]), 0.0)

        pos = p0 + lax.broadcasted_iota(jnp.int32, (tm, 1), 0)
        ys = []
        for g, win in enumerate(POOL_WINDOWS):
            cols = slice(g * POOL_GROUP_DIM, (g + 1) * POOL_GROUP_DIM)
            before, after = win // 2, win - win // 2
            s = None
            for d in range(-before, after):
                term = hbuf[POOL_HALO + d:POOL_HALO + d + tm, cols]
                s = term if s is None else s + term
            lo = jnp.maximum(pos - before, 0)
            hi = jnp.minimum(pos + after, seq_len)
            inv_cnt = 1.0 / (hi - lo).astype(F32)
            pooled = s * inv_cnt - hbuf[POOL_HALO:POOL_HALO + tm, cols]
            ys.append(_mm(pooled, pw_ref[g]))
        return jnp.concatenate(ys, axis=-1) * ps_ref[...]

    _staggered_ffn(front, x_ref, modc_ref, modp_ref, ng_ref, wgu_ref, wd_ref, o_ref,
                   (x1_0, x1_1), (h2_0, h2_1), abuf)


def _pool_ffn(x2d, mod, ng, pw, ps, wgu, wd, layer, *, seq_len, tm):
    t_total = x2d.shape[0]
    n_tiles = t_total // tm
    halo_blocks = t_total // POOL_HALO
    per_tile = tm // POOL_HALO
    cur, prev, mod_specs, weight_specs, scratch = _staggered_specs(
        tm, n_tiles, mod, wgu, wd, layer)
    kern = functools.partial(_pool_ffn_kernel, tm=tm, seq_len=seq_len, n_tiles=n_tiles)
    return pl.pallas_call(
        kern,
        grid=(n_tiles + 1,),
        in_specs=[
            pl.BlockSpec((tm, D_MODEL), lambda t: (cur(t), 0)),
            pl.BlockSpec((POOL_HALO, D_MODEL),
                         lambda t: (jnp.maximum(cur(t) * per_tile - 1, 0), 0)),
            pl.BlockSpec((POOL_HALO, D_MODEL),
                         lambda t: (jnp.minimum((cur(t) + 1) * per_tile, halo_blocks - 1), 0)),
            *mod_specs,
            _const_spec((4, D_MODEL)),
            _const_spec(pw.shape),
            _const_spec((1, D_MODEL)),
            *weight_specs,
        ],
        out_specs=pl.BlockSpec((tm, D_MODEL), lambda t: (prev(t), 0)),
        out_shape=jax.ShapeDtypeStruct(x2d.shape, F32),
        scratch_shapes=[pltpu.VMEM((tm + 2 * POOL_HALO, D_MODEL), F32)] + scratch,
        compiler_params=_params(1),
        name="pool_ffn",
    )(x2d, x2d, x2d, mod, mod, ng, pw, ps, wgu, wd)


def _zero_after(v):
    bits = pltpu.bitcast(v, jnp.int32)
    return lax.shift_right_logical(lax.shift_right_logical(bits, 16), 16).astype(F32)


def _ffn_tail(x1_ref, h2_ref, mod, ng_ref, wgu_ref, wd_ref, abuf, anchors):
    for c in range(FFN_HIDDEN // FFN_CHUNK):
        lo = c * FFN_CHUNK
        g = jnp.dot(h2_ref[...], wgu_ref[:, lo:lo + FFN_CHUNK], preferred_element_type=F32)
        u = jnp.dot(h2_ref[...], wgu_ref[:, FFN_HIDDEN + lo:FFN_HIDDEN + lo + FFN_CHUNK],
                    preferred_element_type=F32)
        if 1 <= c <= len(anchors):
            rows = anchors[c - 1].shape[0]
            u = jnp.concatenate([u[:rows] + anchors[c - 1], u[rows:]], axis=0)
        abuf[:, lo:lo + FFN_CHUNK] = (_silu(g) * u).astype(BF16)
    f = jnp.dot(abuf[...], wd_ref[...], preferred_element_type=F32)
    return x1_ref[...] + _rms(f, ng_ref[3:4] * mod[5:6])


def _staggered_ffn(front, x_ref, modc_ref, modp_ref, ng_ref, wgu_ref, wd_ref, o_ref,
                   x1_sc, h2_sc, abuf):
    t = pl.program_id(0)
    tm = x_ref.shape[0]

    @pl.when(t == 0)
    def _():
        x1_sc[1][...] = jnp.zeros(x1_sc[1].shape, F32)
        h2_sc[1][...] = jnp.zeros(h2_sc[1].shape, BF16)

    def step(cur, prev):
        mod = modc_ref[0]
        y = front(mod)
        group = tm // FFN_NORM_GROUPS
        anchors = []
        for r0 in range(0, tm, group):
            rg = slice(r0, r0 + group)
            x1 = x_ref[rg, :] + _rms(y[rg], ng_ref[1:2] * mod[2:3])
            x1_sc[cur][rg, :] = x1
            h2 = _norm_mod(x1, ng_ref[2:3], mod[3:4], mod[4:5])
            h2_sc[cur][rg, :] = h2.astype(BF16)
            anchors.append(_zero_after(h2[:8, :FFN_CHUNK]))
        o_ref[...] = _ffn_tail(x1_sc[prev], h2_sc[prev], modp_ref[0], ng_ref, wgu_ref, wd_ref, abuf,
                               anchors)

    @pl.when(t % 2 == 0)
    def _():
        step(0, 1)

    @pl.when(t % 2 == 1)
    def _():
        step(1, 0)


def _staggered_specs(tm, n_tiles, mod, wgu, wd, layer):
    tiles_per_mod = n_tiles // mod.shape[0]

    def cur(t):
        return jnp.minimum(t, n_tiles - 1)

    def prev(t):
        return jnp.maximum(t - 1, 0)

    mod_specs = [
        pl.BlockSpec((1, 6, D_MODEL), lambda t: (cur(t) // tiles_per_mod, 0, 0)),
        pl.BlockSpec((1, 6, D_MODEL), lambda t: (prev(t) // tiles_per_mod, 0, 0)),
    ]
    weight_specs = [_layer_spec(wgu.shape, layer), _layer_spec(wd.shape, layer)]
    scratch = ([pltpu.VMEM((tm, D_MODEL), F32)] * 2 + [pltpu.VMEM((tm, D_MODEL), BF16)] * 2
               + [pltpu.VMEM((tm, FFN_HIDDEN), BF16)])
    return cur, prev, mod_specs, weight_specs, scratch


def _proj_ffn_kernel(x_ref, a_ref, modc_ref, modp_ref, ng_ref, wo_ref, wgu_ref, wd_ref, o_ref,
                     x1_0, x1_1, h2_0, h2_1, abuf):
    def front(mod):
        del mod
        return jnp.dot(a_ref[...], wo_ref[...], preferred_element_type=F32)

    _staggered_ffn(front, x_ref, modc_ref, modp_ref, ng_ref, wgu_ref, wd_ref, o_ref,
                   (x1_0, x1_1), (h2_0, h2_1), abuf)


def _proj_ffn(x2d, attn2d, mod, ng, wo, wgu, wd, layer, *, tm):
    n_tiles = x2d.shape[0] // tm
    cur, prev, mod_specs, weight_specs, scratch = _staggered_specs(
        tm, n_tiles, mod, wgu, wd, layer)
    return pl.pallas_call(
        _proj_ffn_kernel,
        grid=(n_tiles + 1,),
        in_specs=[
            pl.BlockSpec((tm, D_MODEL), lambda t: (cur(t), 0)),
            pl.BlockSpec((tm, attn2d.shape[1]), lambda t: (cur(t), 0)),
            *mod_specs,
            _const_spec((4, D_MODEL)),
            _const_spec(wo.shape),
            *weight_specs,
        ],
        out_specs=pl.BlockSpec((tm, D_MODEL), lambda t: (prev(t), 0)),
        out_shape=jax.ShapeDtypeStruct(x2d.shape, F32),
        scratch_shapes=scratch,
        compiler_params=_params(1),
        name="proj_ffn",
    )(x2d, attn2d, mod, mod, ng, wo, wgu, wd)


def _norm_proj_kernel(x_ref, mod_ref, ng_ref, w_ref, o_ref, *, n_chunks, chunk):
    mod = mod_ref[0]
    h = _norm_mod(x_ref[...], ng_ref[0:1], mod[0:1], mod[1:2]).astype(BF16)
    for c in range(n_chunks):
        sl = slice(c * chunk, (c + 1) * chunk)
        o_ref[:, sl] = jnp.dot(h, w_ref[:, sl], preferred_element_type=F32).astype(BF16)


def _norm_proj(x2d, mod, ng, w, *, tm):
    n_tiles = x2d.shape[0] // tm
    tiles_per_mod = n_tiles // mod.shape[0]
    n_out = w.shape[1]
    chunk = 1024
    kern = functools.partial(_norm_proj_kernel, n_chunks=n_out // chunk, chunk=chunk)
    return pl.pallas_call(
        kern,
        grid=(n_tiles,),
        in_specs=[
            pl.BlockSpec((tm, D_MODEL), lambda t: (t, 0)),
            pl.BlockSpec((1, 6, D_MODEL), lambda t: (t // tiles_per_mod, 0, 0)),
            _const_spec((4, D_MODEL)),
            _const_spec(w.shape),
        ],
        out_specs=pl.BlockSpec((tm, n_out), lambda t: (t, 0)),
        out_shape=jax.ShapeDtypeStruct((x2d.shape[0], n_out), BF16),
        compiler_params=_params(1),
        name="norm_proj",
    )(x2d, mod, ng, w)


def _rope(x, c, sa, sb):
    return x * c + pltpu.roll(x, 32, 1) * sa + pltpu.roll(x, 96, 1) * sb


def _mla_qkv_kernel(x_ref, mod_ref, ng_ref, win_ref, qn_ref, kvn_ref, wqb_ref, wkvb_ref,
                    c_ref, sa_ref, sb_ref, q_ref, k_ref, v_ref):
    mod = mod_ref[0]
    h = _norm_mod(x_ref[0], ng_ref[0:1], mod[0:1], mod[1:2])
    a = _mm(h, win_ref[...])
    q_pre = _mm(_rms(a[:, :MLA_Q_LORA], qn_ref[...]), wqb_ref[...])
    kv = _mm(_rms(a[:, MLA_Q_LORA:MLA_Q_LORA + MLA_KV_LORA], kvn_ref[...]), wkvb_ref[...])
    c, sa, sb = c_ref[...], sa_ref[...], sb_ref[...]
    k_rope = _rope(a[:, MLA_Q_LORA + MLA_KV_LORA:], c, sa, sb)
    v0 = MLA_HEADS * MLA_NOPE
    for hd in range(MLA_HEADS):
        q0 = hd * MLA_QK_PAD
        q_rope = _rope(q_pre[:, q0 + MLA_NOPE:q0 + MLA_QK_PAD], c, sa, sb)
        q_ref[0, hd] = jnp.concatenate(
            [q_pre[:, q0:q0 + MLA_NOPE], q_rope], axis=-1).astype(BF16)
        k_ref[0, hd] = jnp.concatenate(
            [kv[:, hd * MLA_NOPE:(hd + 1) * MLA_NOPE], k_rope], axis=-1).astype(BF16)
        v_ref[0, hd] = kv[:, v0 + hd * MLA_V:v0 + (hd + 1) * MLA_V].astype(BF16)


def _mla_qkv(x3d, mod, ng, win, qn, kvn, wqb, wkvb, tabs, *, tm):
    b, s, _ = x3d.shape
    per_batch = mod.shape[0] == b
    tab_spec = pl.BlockSpec((tm, LANES), lambda i, t: (t, 0))
    qk_shape = jax.ShapeDtypeStruct((b, MLA_HEADS, s, MLA_QK_PAD), BF16)
    return pl.pallas_call(
        _mla_qkv_kernel,
        grid=(b, s // tm),
        in_specs=[
            pl.BlockSpec((1, tm, D_MODEL), lambda i, t: (i, t, 0)),
            pl.BlockSpec((1, 6, D_MODEL), lambda i, t: (i if per_batch else 0, 0, 0)),
            _const_spec((4, D_MODEL)),
            _const_spec(win.shape),
            _const_spec(qn.shape),
            _const_spec(kvn.shape),
            _const_spec(wqb.shape),
            _const_spec(wkvb.shape),
            tab_spec, tab_spec, tab_spec,
        ],
        out_specs=[
            pl.BlockSpec((1, MLA_HEADS, tm, MLA_QK_PAD), lambda i, t: (i, 0, t, 0)),
            pl.BlockSpec((1, MLA_HEADS, tm, MLA_QK_PAD), lambda i, t: (i, 0, t, 0)),
            pl.BlockSpec((1, MLA_HEADS, tm, MLA_V), lambda i, t: (i, 0, t, 0)),
        ],
        out_shape=[qk_shape, qk_shape, jax.ShapeDtypeStruct((b, MLA_HEADS, s, MLA_V), BF16)],
        compiler_params=_params(2),
        name="mla_qkv",
    )(x3d, mod, ng, win, qn, kvn, wqb, wkvb, *tabs)


def _softmax_chunk(q, k, v, scale):
    s = _mm_nt(q, k)
    m = jnp.max(s, axis=-1, keepdims=True)
    p = jnp.exp((s - m) * scale)
    return m, jnp.sum(p, axis=-1, keepdims=True), jnp.dot(
        p.astype(BF16), v, preferred_element_type=F32)


def _lane_tiles(x):
    return [x[:, t * LANES:(t + 1) * LANES] for t in range(x.shape[1] // LANES)]


def _mla_attn_kernel(q_ref, kc_ref, vc_ref, kl_ref, vl_ref, o_ref,
                     vx_sc, s_0, s_1, p_0, p_1, m_0, m_1, *, tq, kv_chunk, pv_chunk):
    s_sc, p_sc, m_sc = (s_0, s_1), (p_0, p_1), (m_0, m_1)
    lc, l = kc_ref.shape[2], kl_ref.shape[2]
    n_tiles = l // tq
    exp2_scale = MLA_SCALE * LOG2E
    k_chunks = [(kc_ref, c0) for c0 in range(0, lc, kv_chunk)]
    k_chunks += [(kl_ref, c0) for c0 in range(0, l, kv_chunk)]
    pv_ends = {lc: 0, **{lc + c0 + pv_chunk: lc + c0 for c0 in range(0, l, pv_chunk)}}

    vx_sc[:lc, :MLA_V] = vc_ref[0, 0]
    vx_sc[lc:, :MLA_V] = vl_ref[0, 0]
    vx_sc[:, MLA_V:] = jnp.ones((lc + l, LANES), BF16)

    def rows(tile):
        start = tile * tq
        return pl.ds(start if isinstance(start, int) else pl.multiple_of(start, tq), tq)

    def stage(t, parity):
        do_values = not isinstance(t, int) or 0 <= t - 1 < n_tiles
        do_scores = not isinstance(t, int) or 0 <= t + 1 < n_tiles
        do_exps = not isinstance(t, int) or 0 <= t < n_tiles
        other = 1 - parity
        if do_scores:
            q = q_ref[0, 0, rows(t + 1), :]
        m_part = acc = None
        for idx, (k_ref, c0) in enumerate(k_chunks):
            cols = slice(idx * kv_chunk, (idx + 1) * kv_chunk)
            if do_exps:
                for t0 in range(cols.start, cols.stop, LANES):
                    lanes = slice(t0, t0 + LANES)
                    p_sc[parity][:, lanes] = jnp.exp2(
                        (s_sc[parity][:, lanes] - m_sc[parity][...]) * exp2_scale).astype(BF16)
            if do_scores:
                s = _mm_nt(q, k_ref[0, 0, c0:c0 + kv_chunk, :])
                s_sc[other][:, cols] = s
                for s_tile in _lane_tiles(s):
                    m_part = s_tile if m_part is None else jnp.maximum(m_part, s_tile)
            if do_values and cols.stop in pv_ends:
                keys = slice(pv_ends[cols.stop], cols.stop)
                pv = jnp.dot(p_sc[other][:, keys], vx_sc[keys, :], preferred_element_type=F32)
                acc = pv if acc is None else acc + pv
        if do_scores:
            m_sc[other][...] = jnp.broadcast_to(
                jnp.max(m_part, axis=-1, keepdims=True), (tq, LANES))
        if do_values:
            o_ref[0, rows(t - 1), :] = (acc[:, :MLA_V] / acc[:, MLA_V:]).astype(BF16)

    stage(-1, 1)
    stage(0, 0)

    def body(i, carry):
        stage(2 * i + 1, 1)
        stage(2 * i + 2, 0)
        return carry

    lax.fori_loop(0, (n_tiles - 2) // 2, body, 0)
    stage(n_tiles - 1, (n_tiles - 1) % 2)
    stage(n_tiles, n_tiles % 2)


def _mla_attn(q, kc, vc, kl, vl, *, tq, kv_chunk):
    b, h, l, _ = q.shape
    lc = kc.shape[2]
    kern = functools.partial(_mla_attn_kernel, tq=tq, kv_chunk=kv_chunk, pv_chunk=4 * kv_chunk)
    return pl.pallas_call(
        kern,
        grid=(b, h),
        in_specs=[
            pl.BlockSpec((1, 1, l, MLA_QK_PAD), lambda i, j: (i, j, 0, 0)),
            pl.BlockSpec((1, 1, lc, MLA_QK_PAD), lambda i, j: (i, j, 0, 0)),
            pl.BlockSpec((1, 1, lc, MLA_V), lambda i, j: (i, j, 0, 0)),
            pl.BlockSpec((1, 1, l, MLA_QK_PAD), lambda i, j: (i, j, 0, 0)),
            pl.BlockSpec((1, 1, l, MLA_V), lambda i, j: (i, j, 0, 0)),
        ],
        out_specs=pl.BlockSpec((1, l, MLA_V), lambda i, j: (i, 0, j)),
        out_shape=jax.ShapeDtypeStruct((b, l, h * MLA_V), BF16),
        scratch_shapes=(
            [pltpu.VMEM((lc + l, 2 * LANES), BF16)] + [pltpu.VMEM((tq, lc + l), F32)] * 2
            + [pltpu.VMEM((tq, lc + l), BF16)] * 2 + [pltpu.VMEM((tq, LANES), F32)] * 2),
        compiler_params=_params(2),
        name="mla_attn",
    )(q, kc, vc, kl, vl)


def _ctx_attn_kernel(q_ref, k_ref, v_ref, o_ref):
    for hd in range(q_ref.shape[1]):
        _, l, acc = _softmax_chunk(q_ref[0, hd], k_ref[0, hd], v_ref[0, hd], MLA_SCALE)
        o_ref[0, :, hd * MLA_V:(hd + 1) * MLA_V] = (acc / l).astype(BF16)


def _ctx_attn(q, k, v):
    b, h, lc, _ = q.shape
    qk_spec = pl.BlockSpec((1, h, lc, MLA_QK_PAD), lambda i: (i, 0, 0, 0))
    return pl.pallas_call(
        _ctx_attn_kernel,
        grid=(b,),
        in_specs=[qk_spec, qk_spec, pl.BlockSpec((1, h, lc, MLA_V), lambda i: (i, 0, 0, 0))],
        out_specs=pl.BlockSpec((1, lc, h * MLA_V), lambda i: (i, 0, 0)),
        out_shape=jax.ShapeDtypeStruct((b, lc, h * MLA_V), BF16),
        compiler_params=_params(1),
        name="mla_ctx_attn",
    )(q, k, v)


def _na_attn_kernel(q_ref, k_ref, v_ref, kc_ref, vc_ref, tab_ref, o_ref,
                    vx_sc, vcx_sc, z_0, z_1, p_0, p_1, m_0, m_1, *, rows):
    z_sc, p_sc, m_sc = (z_0, z_1), (p_0, p_1), (m_0, m_1)
    tq = NA_ROW_BLOCK * GRID_W
    n_loc = NA_WIN_ROWS * GRID_W
    n_blocks = rows // NA_ROW_BLOCK
    exp2_scale = NA_SCALE * LOG2E
    lane_head = lax.broadcasted_iota(jnp.int32, (1, LANES), 1) // NA_HEAD_DIM
    key_row = lax.broadcasted_iota(jnp.int32, (1, n_loc), 1) // GRID_W

    vx_sc[:, :LANES] = v_ref[...]
    vx_sc[:, LANES:] = jnp.ones(v_ref.shape, BF16)
    vcx_sc[:, :LANES] = vc_ref[...]
    vcx_sc[:, LANES:] = jnp.ones(vc_ref.shape, BF16)

    def window(j):
        win0 = jnp.clip(j * NA_ROW_BLOCK - NA_ROWS // 2, 0, rows - NA_WIN_ROWS)
        return win0, pl.ds(pl.multiple_of(win0 * GRID_W, GRID_W), n_loc)

    def q_rows(j):
        start = j * tq
        return pl.ds(start if isinstance(start, int) else pl.multiple_of(start, tq), tq)

    def scores(j, slot):
        win0, key_rows = window(j)
        q = q_ref[q_rows(j), :]
        zero = jnp.zeros_like(q)
        q2 = jnp.concatenate(
            [jnp.where(lane_head == 0, q, zero), jnp.where(lane_head == 1, q, zero)], axis=0)
        s_loc = _mm_nt(q2, k_ref[key_rows, :])
        z_ctx = _mm_nt(q2, kc_ref[...]) * exp2_scale
        z_sc[slot][:, n_loc:] = z_ctx
        m_ctx = jnp.max(z_ctx, axis=-1, keepdims=True)
        for hh in range(2):
            for i in range(NA_ROW_BLOCK):
                r = j * NA_ROW_BLOCK + i
                k_lo = jnp.clip(r - NA_ROWS // 2, 0, rows - NA_ROWS) - win0
                row_mask = jnp.where((key_row >= k_lo) & (key_row < k_lo + NA_ROWS), 0.0, NEG)
                tiles = []
                for kp in range(NA_WIN_ROWS // 2):
                    dr0 = win0 + 2 * kp - r + NA_ROWS - 1
                    tiles.append(tab_ref[hh, jnp.clip(dr0 + 1, 0, 2 * NA_ROWS - 1)])
                band = slice(hh * tq + i * GRID_W, hh * tq + (i + 1) * GRID_W)
                z = s_loc[band, :] * exp2_scale + (jnp.concatenate(tiles, axis=-1) + row_mask)
                z_sc[slot][band, :n_loc] = z
                m_sc[slot][band, :] = jnp.maximum(
                    jnp.max(z, axis=-1, keepdims=True), m_ctx[band, :])

    def exps(slot):
        p_sc[slot][...] = jnp.exp2(z_sc[slot][...] - m_sc[slot][...]).astype(BF16)

    def values(j, slot):
        _, key_rows = window(j)
        acc = (jnp.dot(p_sc[slot][:, :n_loc], vx_sc[key_rows, :], preferred_element_type=F32)
               + jnp.dot(p_sc[slot][:, n_loc:], vcx_sc[...], preferred_element_type=F32))
        out = acc[:, :LANES] / acc[:, LANES:]
        o_ref[q_rows(j), :] = jnp.where(lane_head == 0, out[:tq], out[tq:]).astype(BF16)

    def stage(j, parity):
        if not isinstance(j, int) or 0 <= j - 1 < n_blocks:
            values(j - 1, 1 - parity)
        if not isinstance(j, int) or 0 <= j + 1 < n_blocks:
            scores(j + 1, 1 - parity)
        if not isinstance(j, int) or 0 <= j < n_blocks:
            exps(parity)

    stage(-1, 1)
    stage(0, 0)

    def body(t, carry):
        stage(2 * t + 1, 1)
        stage(2 * t + 2, 0)
        return carry

    lax.fori_loop(0, (n_blocks - 2) // 2, body, 0)
    stage(n_blocks - 1, (n_blocks - 1) % 2)
    stage(n_blocks, n_blocks % 2)


def _na_attn(qkv, kvc, tab, *, batch, seq_len, ctx_len):
    rows = seq_len // GRID_W
    n_pairs = NA_WIDTH // LANES
    tq = NA_ROW_BLOCK * GRID_W
    n_keys = NA_WIN_ROWS * GRID_W + ctx_len
    kern = functools.partial(_na_attn_kernel, rows=rows)
    return pl.pallas_call(
        kern,
        grid=(n_pairs, batch),
        in_specs=[
            pl.BlockSpec((seq_len, LANES), lambda p, b: (b, p)),
            pl.BlockSpec((seq_len, LANES), lambda p, b: (b, n_pairs + p)),
            pl.BlockSpec((seq_len, LANES), lambda p, b: (b, 2 * n_pairs + p)),
            pl.BlockSpec((ctx_len, LANES), lambda p, b: (b, p)),
            pl.BlockSpec((ctx_len, LANES), lambda p, b: (b, n_pairs + p)),
            pl.BlockSpec((2, 2 * NA_ROWS, GRID_W, LANES), lambda p, b: (p, 0, 0, 0)),
        ],
        out_specs=pl.BlockSpec((seq_len, LANES), lambda p, b: (b, p)),
        out_shape=jax.ShapeDtypeStruct((batch * seq_len, NA_WIDTH), BF16),
        scratch_shapes=(
            [pltpu.VMEM((seq_len, 2 * LANES), BF16), pltpu.VMEM((ctx_len, 2 * LANES), BF16)]
            + [pltpu.VMEM((2 * tq, n_keys), F32)] * 2 + [pltpu.VMEM((2 * tq, n_keys), BF16)] * 2
            + [pltpu.VMEM((2 * tq, 1), F32)] * 2),
        compiler_params=_params(2),
        name="na_attn",
    )(qkv, qkv, qkv, kvc, kvc, tab)


def _na_bias_table(rpb):
    cols = np.arange(GRID_W)
    col_start = np.clip(cols - NA_COLS // 2, 0, GRID_W - NA_COLS)
    col_mask = (cols[None, :] >= col_start[:, None]) & (cols[None, :] < col_start[:, None] + NA_COLS)
    dc_idx = np.clip(cols[None, :] - cols[:, None] + NA_COLS - 1, 0, 2 * NA_COLS - 2)
    t = jnp.where(col_mask[None, None], rpb.astype(F32)[:, :, dc_idx] * LOG2E, NEG)
    zero = jnp.zeros_like(t[:, :1])
    left = jnp.concatenate([zero, t], axis=1)
    right = jnp.concatenate([t, zero], axis=1)
    return jnp.concatenate([left, right], axis=-1)


def _rope_tables(seq_len):
    t = np.arange(seq_len)
    row = (t // GRID_W).astype(np.float32)
    col = (t % GRID_W).astype(np.float32)
    half = MLA_ROPE // 2
    inv = jnp.asarray(ROPE_BASE, F32) ** (-jnp.arange(0, half, 2, dtype=F32) / half)
    ang = jnp.concatenate([row[:, None] * inv, col[:, None] * inv], axis=-1)
    cos, sin = jnp.cos(ang), jnp.sin(ang)
    z32 = jnp.zeros_like(cos)
    z64 = jnp.zeros((seq_len, 64), F32)
    return (jnp.concatenate([cos, cos, z64], axis=-1),
            jnp.concatenate([z32, sin, z64], axis=-1),
            jnp.concatenate([-sin, z32, z64], axis=-1))


def _identity_rope_tables(seq_len):
    ones = jnp.ones((seq_len, 64), F32)
    z = jnp.zeros((seq_len, 64), F32)
    c = jnp.concatenate([ones, z], axis=-1)
    zz = jnp.zeros((seq_len, LANES), F32)
    return c, zz, zz


def kernel(x, c, ctx, c_ctx, ada_w, ada_b, norm_g, ffn_w_gu, ffn_w_down, pool_w, pool_scale,
           mla_w_in, mla_q_norm, mla_kv_norm, mla_w_qb, mla_w_kvb, mla_w_o, na_w_in, na_rpb,
           na_w_o):
    batch, seq_len, d = x.shape
    ctx_len = ctx.shape[1]
    tm = 512

    n_cond = -(-(batch + 1) // 8) * 8
    cond = jnp.zeros((n_cond, d), F32).at[:batch].set(c).at[batch].set(c_ctx)
    mod_all = _adaln(cond, ada_w, ada_b).reshape(DEPTH, n_cond, 6, d)

    wgu = ffn_w_gu.astype(BF16)
    wd = ffn_w_down.astype(BF16)

    xl = x.reshape(batch * seq_len, d)
    xc = ctx.reshape(batch * ctx_len, d)

    for i in range(DEPTH):
        kind = i % N_MIXERS
        j = i // N_MIXERS
        ctx_after = any(l % N_MIXERS != MIX_POOL for l in range(i + 1, DEPTH))
        mod_lat = mod_all[i, :batch]
        mod_ctx = mod_all[i, batch:batch + 1]
        ng = norm_g[i]

        if kind == MIX_POOL:
            pw = pool_w[j].astype(BF16)
            ps = pool_scale[j].reshape(1, d)
            xl = _pool_ffn(xl, mod_lat, ng, pw, ps, wgu, wd, i, seq_len=seq_len, tm=tm)
            if ctx_after:
                xc = _pool_ffn(xc, mod_ctx, ng, pw, ps, wgu, wd, i, seq_len=ctx_len, tm=ctx_len)
        elif kind == MIX_MLA:
            win = jnp.pad(mla_w_in[j], ((0, 0), (0, 64))).astype(BF16)
            wqb = jnp.pad(
                mla_w_qb[j].reshape(MLA_Q_LORA, MLA_HEADS, MLA_NOPE + MLA_ROPE),
                ((0, 0), (0, 0), (0, MLA_QK_PAD - MLA_NOPE - MLA_ROPE)),
            ).reshape(MLA_Q_LORA, MLA_HEADS * MLA_QK_PAD).astype(BF16)
            wkvb = mla_w_kvb[j].reshape(MLA_KV_LORA, MLA_HEADS, 2, MLA_NOPE).transpose(
                0, 2, 1, 3).reshape(MLA_KV_LORA, 2 * MLA_HEADS * MLA_NOPE).astype(BF16)
            qn = mla_q_norm[j].reshape(1, MLA_Q_LORA)
            kvn = mla_kv_norm[j].reshape(1, MLA_KV_LORA)
            wo = mla_w_o[j].astype(BF16)
            ql, kl, vl = _mla_qkv(xl.reshape(batch, seq_len, d), mod_lat, ng, win, qn, kvn, wqb,
                                  wkvb, _rope_tables(seq_len), tm=tm)
            qc, kc, vc = _mla_qkv(xc.reshape(batch, ctx_len, d), mod_ctx, ng, win, qn, kvn, wqb,
                                  wkvb, _identity_rope_tables(ctx_len), tm=ctx_len)
            a_lat = _mla_attn(ql, kc, vc, kl, vl, tq=512, kv_chunk=256)
            xl = _proj_ffn(xl, a_lat.reshape(batch * seq_len, -1), mod_lat, ng, wo, wgu, wd, i,
                           tm=tm)
            if ctx_after:
                a_ctx = _ctx_attn(qc, kc, vc)
                xc = _proj_ffn(xc, a_ctx.reshape(batch * ctx_len, -1), mod_ctx, ng, wo, wgu, wd, i,
                               tm=ctx_len)
        else:
            w_in = na_w_in[j].astype(BF16)
            wo = na_w_o[j].astype(BF16)
            qkv = _norm_proj(xl, mod_lat, ng, w_in, tm=tm)
            kvc = _norm_proj(xc, mod_ctx, ng, w_in[:, NA_WIDTH:], tm=ctx_len)
            a_lat = _na_attn(qkv, kvc, _na_bias_table(na_rpb[j]), batch=batch, seq_len=seq_len,
                             ctx_len=ctx_len)
            xl = _proj_ffn(xl, a_lat, mod_lat, ng, wo, wgu, wd, i, tm=tm)
            if ctx_after:
                raise NotImplementedError("context output of a neighbourhood layer")
    return xl.reshape(batch, seq_len, d)
```

```python
import functools

import jax
import jax.numpy as jnp
import numpy as np
from jax import lax
from jax.experimental import pallas as pl
from jax.experimental.pallas import tpu as pltpu

D_MODEL = 1024
DEPTH = 4
GRID_W = 64
N_MIXERS = 3
MIX_POOL, MIX_MLA, MIX_NA = 0, 1, 2
RMS_EPS = 1e-6

POOL_WINDOWS = (2, 4, 8, 16)
POOL_GROUP_DIM = D_MODEL // len(POOL_WINDOWS)
POOL_HALO = 8

MLA_HEADS = 8
MLA_NOPE = 128
MLA_ROPE = 64
MLA_V = 128
MLA_Q_LORA = 384
MLA_KV_LORA = 256
MLA_QK_PAD = 256
MLA_SCALE = (MLA_NOPE + MLA_ROPE) ** -0.5
ROPE_BASE = 10000.0

NA_HEADS = 16
NA_HEAD_DIM = 64
NA_WIDTH = NA_HEADS * NA_HEAD_DIM
NA_ROWS = 8
NA_COLS = 16
NA_SCALE = NA_HEAD_DIM ** -0.5
NA_ROW_BLOCK = 4
NA_WIN_ROWS = NA_ROW_BLOCK + NA_ROWS

FFN_HIDDEN = 2816
FFN_CHUNK = 256

FFN_TOKEN_TILE = 512
PROJ_TOKEN_TILE = 1024
MLA_Q_TILE = 512
MLA_KV_CHUNK = 256
FFN_NORM_GROUPS = 8

LANES = 128
VMEM_LIMIT_BYTES = 56 * 1024 * 1024

NEG = -1e30
LOG2E = float(np.log2(np.e))

BF16 = jnp.bfloat16
F32 = jnp.float32


def _params(n_axes, flags=None):
    return pltpu.CompilerParams(
        dimension_semantics=("arbitrary",) * n_axes, vmem_limit_bytes=VMEM_LIMIT_BYTES,
        flags=flags)


def _const_spec(shape):
    nd = len(shape)
    return pl.BlockSpec(shape, lambda *_: (0,) * nd, pipeline_mode=pl.Buffered(1))


def _layer_spec(stacked_shape, layer):
    rest = tuple(stacked_shape[1:])
    return pl.BlockSpec((None,) + rest, lambda *_: (layer,) + (0,) * len(rest),
                        pipeline_mode=pl.Buffered(1))


def _rms(xf, g):
    return xf * lax.rsqrt(jnp.mean(xf * xf, axis=-1, keepdims=True) + RMS_EPS) * g


def _norm_mod(xf, g, shift, scale):
    return _rms(xf, g * (1.0 + scale)) + shift


def _mm(a, b):
    return jnp.dot(a.astype(BF16), b, preferred_element_type=F32)


def _mm_nt(a, b):
    return lax.dot_general(a, b, (((1,), (1,)), ((), ())), preferred_element_type=F32)


def _silu(x):
    return x / (1.0 + jnp.exp(-x))


def _adaln_kernel(cond_ref, w_ref, b_ref, o_ref):
    cond = cond_ref[...]
    o_ref[0] = _mm(_silu(cond), w_ref[0].astype(BF16)) + b_ref[0]


def _adaln(cond, ada_w, ada_b):
    rows = cond.shape[0]
    tn = 1024
    n_out = ada_w.shape[-1]
    return pl.pallas_call(
        _adaln_kernel,
        grid=(DEPTH, n_out // tn),
        in_specs=[
            pl.BlockSpec((rows, D_MODEL), lambda l, n: (0, 0)),
            pl.BlockSpec((1, D_MODEL, tn), lambda l, n: (l, 0, n)),
            pl.BlockSpec((1, 1, tn), lambda l, n: (l, 0, n)),
        ],
        out_specs=pl.BlockSpec((1, rows, tn), lambda l, n: (l, 0, n)),
        out_shape=jax.ShapeDtypeStruct((DEPTH, rows, n_out), F32),
        compiler_params=_params(2),
        name="adaln",
    )(cond, ada_w, ada_b.reshape(DEPTH, 1, n_out))


def _pool_ffn_kernel(x_ref, xp_ref, xn_ref, modc_ref, modp_ref, ng_ref, pw_ref, ps_ref, wgu_ref,
                     wd_ref, o_ref, hbuf, x1_0, x1_1, h2_0, h2_1, abuf, *, tm, seq_len, n_tiles):
    tiles_per_seq = seq_len // tm
    tile = jnp.minimum(pl.program_id(0), n_tiles - 1)
    p0 = (tile % tiles_per_seq) * tm

    def front(mod):
        def hfun(xv):
            return _norm_mod(xv, ng_ref[0:1], mod[0:1], mod[1:2])

        hbuf[0:POOL_HALO, :] = jnp.where(p0 > 0, hfun(xp_ref[...]), 0.0)
        hbuf[POOL_HALO:POOL_HALO + tm, :] = hfun(x_ref[...])
        hbuf[POOL_HALO + tm:, :] = jnp.where(p0 + tm < seq_len, hfun(xn_ref[...]), 0.0)

        pos = p0 + lax.broadcasted_iota(jnp.int32, (tm, 1), 0)
        ys = []
        for g, win in enumerate(POOL_WINDOWS):
            cols = slice(g * POOL_GROUP_DIM, (g + 1) * POOL_GROUP_DIM)
            before, after = win // 2, win - win // 2
            s = None
            for d in range(-before, after):
                term = hbuf[POOL_HALO + d:POOL_HALO + d + tm, cols]
                s = term if s is None else s + term
            lo = jnp.maximum(pos - before, 0)
            hi = jnp.minimum(pos + after, seq_len)
            inv_cnt = 1.0 / (hi - lo).astype(F32)
            pooled = s * inv_cnt - hbuf[POOL_HALO:POOL_HALO + tm, cols]
            ys.append(_mm(pooled, pw_ref[g]))
        return jnp.concatenate(ys, axis=-1) * ps_ref[...]

    _staggered_ffn(front, x_ref, modc_ref, modp_ref, ng_ref, wgu_ref, wd_ref, o_ref,
                   (x1_0, x1_1), (h2_0, h2_1), abuf)


def _pool_ffn(x2d, mod, ng, pw, ps, wgu, wd, layer, *, seq_len, tm):
    t_total = x2d.shape[0]
    n_tiles = t_total // tm
    halo_blocks = t_total // POOL_HALO
    per_tile = tm // POOL_HALO
    cur, prev, mod_specs, weight_specs, scratch = _staggered_specs(
        tm, n_tiles, mod, wgu, wd, layer)
    kern = functools.partial(_pool_ffn_kernel, tm=tm, seq_len=seq_len, n_tiles=n_tiles)
    return pl.pallas_call(
        kern,
        grid=(n_tiles + 1,),
        in_specs=[
            pl.BlockSpec((tm, D_MODEL), lambda t: (cur(t), 0)),
            pl.BlockSpec((POOL_HALO, D_MODEL),
                         lambda t: (jnp.maximum(cur(t) * per_tile - 1, 0), 0)),
            pl.BlockSpec((POOL_HALO, D_MODEL),
                         lambda t: (jnp.minimum((cur(t) + 1) * per_tile, halo_blocks - 1), 0)),
            *mod_specs,
            _const_spec((4, D_MODEL)),
            _const_spec(pw.shape),
            _const_spec((1, D_MODEL)),
            *weight_specs,
        ],
        out_specs=pl.BlockSpec((tm, D_MODEL), lambda t: (prev(t), 0)),
        out_shape=jax.ShapeDtypeStruct(x2d.shape, F32),
        scratch_shapes=[pltpu.VMEM((tm + 2 * POOL_HALO, D_MODEL), F32)] + scratch,
        compiler_params=_params(1),
        name="pool_ffn",
    )(x2d, x2d, x2d, mod, mod, ng, pw, ps, wgu, wd)


def _zero_after(v):
    bits = pltpu.bitcast(v, jnp.int32)
    return lax.shift_right_logical(lax.shift_right_logical(bits, 16), 16).astype(F32)


def _ffn_tail(x1_ref, h2_ref, mod, ng_ref, wgu_ref, wd_ref, abuf, anchors):
    for c in range(FFN_HIDDEN // FFN_CHUNK):
        lo = c * FFN_CHUNK
        g = jnp.dot(h2_ref[...], wgu_ref[:, lo:lo + FFN_CHUNK], preferred_element_type=F32)
        u = jnp.dot(h2_ref[...], wgu_ref[:, FFN_HIDDEN + lo:FFN_HIDDEN + lo + FFN_CHUNK],
                    preferred_element_type=F32)
        if 1 <= c <= len(anchors):
            rows = anchors[c - 1].shape[0]
            u = jnp.concatenate([u[:rows] + anchors[c - 1], u[rows:]], axis=0)
        abuf[:, lo:lo + FFN_CHUNK] = (_silu(g) * u).astype(BF16)
    f = jnp.dot(abuf[...], wd_ref[...], preferred_element_type=F32)
    return x1_ref[...] + _rms(f, ng_ref[3:4] * mod[5:6])


def _staggered_ffn(front, x_ref, modc_ref, modp_ref, ng_ref, wgu_ref, wd_ref, o_ref,
                   x1_sc, h2_sc, abuf):
    t = pl.program_id(0)
    last = pl.num_programs(0) - 1
    tm = x_ref.shape[0]

    def start(cur):
        mod = modc_ref[0]
        y = front(mod)
        group = tm // FFN_NORM_GROUPS
        anchors = []
        for r0 in range(0, tm, group):
            rg = slice(r0, r0 + group)
            x1 = x_ref[rg, :] + _rms(y[rg], ng_ref[1:2] * mod[2:3])
            x1_sc[cur][rg, :] = x1
            h2 = _norm_mod(x1, ng_ref[2:3], mod[3:4], mod[4:5])
            h2_sc[cur][rg, :] = h2.astype(BF16)
            anchors.append(_zero_after(h2[:8, :FFN_CHUNK]))
        return anchors

    def finish(prev, anchors):
        o_ref[...] = _ffn_tail(x1_sc[prev], h2_sc[prev], modp_ref[0], ng_ref, wgu_ref, wd_ref, abuf,
                               anchors)

    @pl.when(t == 0)
    def _():
        start(0)

    for parity in (0, 1):
        @pl.when((t > 0) & (t < last) & (t % 2 == parity))
        def _(parity=parity):
            finish(1 - parity, start(parity))

        @pl.when((t == last) & (t % 2 == parity))
        def _(parity=parity):
            finish(1 - parity, [])


def _staggered_specs(tm, n_tiles, mod, wgu, wd, layer):
    tiles_per_mod = n_tiles // mod.shape[0]

    def cur(t):
        return jnp.minimum(t, n_tiles - 1)

    def prev(t):
        return jnp.maximum(t - 1, 0)

    mod_specs = [
        pl.BlockSpec((1, 6, D_MODEL), lambda t: (cur(t) // tiles_per_mod, 0, 0)),
        pl.BlockSpec((1, 6, D_MODEL), lambda t: (prev(t) // tiles_per_mod, 0, 0)),
    ]
    weight_specs = [_layer_spec(wgu.shape, layer), _layer_spec(wd.shape, layer)]
    scratch = ([pltpu.VMEM((tm, D_MODEL), F32)] * 2 + [pltpu.VMEM((tm, D_MODEL), BF16)] * 2
               + [pltpu.VMEM((tm, FFN_HIDDEN), BF16)])
    return cur, prev, mod_specs, weight_specs, scratch


def _proj_ffn_kernel(x_ref, a_ref, modc_ref, modp_ref, ng_ref, wo_ref, wgu_ref, wd_ref, o_ref,
                     x1_0, x1_1, h2_0, h2_1, abuf):
    def front(mod):
        del mod
        return jnp.dot(a_ref[...], wo_ref[...], preferred_element_type=F32)

    _staggered_ffn(front, x_ref, modc_ref, modp_ref, ng_ref, wgu_ref, wd_ref, o_ref,
                   (x1_0, x1_1), (h2_0, h2_1), abuf)


def _proj_ffn(x2d, attn2d, mod, ng, wo, wgu, wd, layer, *, tm):
    n_tiles = x2d.shape[0] // tm
    cur, prev, mod_specs, weight_specs, scratch = _staggered_specs(
        tm, n_tiles, mod, wgu, wd, layer)
    return pl.pallas_call(
        _proj_ffn_kernel,
        grid=(n_tiles + 1,),
        in_specs=[
            pl.BlockSpec((tm, D_MODEL), lambda t: (cur(t), 0)),
            pl.BlockSpec((tm, attn2d.shape[1]), lambda t: (cur(t), 0)),
            *mod_specs,
            _const_spec((4, D_MODEL)),
            _const_spec(wo.shape),
            *weight_specs,
        ],
        out_specs=pl.BlockSpec((tm, D_MODEL), lambda t: (prev(t), 0)),
        out_shape=jax.ShapeDtypeStruct(x2d.shape, F32),
        scratch_shapes=scratch,
        compiler_params=_params(1),
        name="proj_ffn",
    )(x2d, attn2d, mod, mod, ng, wo, wgu, wd)


def _norm_proj_kernel(x_ref, mod_ref, ng_ref, w_ref, o_ref, *, n_chunks, chunk):
    mod = mod_ref[0]
    h = _norm_mod(x_ref[...], ng_ref[0:1], mod[0:1], mod[1:2]).astype(BF16)
    for c in range(n_chunks):
        sl = slice(c * chunk, (c + 1) * chunk)
        o_ref[:, sl] = jnp.dot(h, w_ref[:, sl], preferred_element_type=F32).astype(BF16)


def _norm_proj(x2d, mod, ng, w, *, tm):
    n_tiles = x2d.shape[0] // tm
    tiles_per_mod = n_tiles // mod.shape[0]
    n_out = w.shape[1]
    chunk = 1024
    kern = functools.partial(_norm_proj_kernel, n_chunks=n_out // chunk, chunk=chunk)
    return pl.pallas_call(
        kern,
        grid=(n_tiles,),
        in_specs=[
            pl.BlockSpec((tm, D_MODEL), lambda t: (t, 0)),
            pl.BlockSpec((1, 6, D_MODEL), lambda t: (t // tiles_per_mod, 0, 0)),
            _const_spec((4, D_MODEL)),
            _const_spec(w.shape),
        ],
        out_specs=pl.BlockSpec((tm, n_out), lambda t: (t, 0)),
        out_shape=jax.ShapeDtypeStruct((x2d.shape[0], n_out), BF16),
        compiler_params=_params(1),
        name="norm_proj",
    )(x2d, mod, ng, w)


def _rope(x, c, sa, sb):
    return x * c + pltpu.roll(x, 32, 1) * sa + pltpu.roll(x, 96, 1) * sb


def _mla_qkv_kernel(x_ref, mod_ref, ng_ref, win_ref, qn_ref, kvn_ref, wqb_ref, wkvb_ref,
                    c_ref, sa_ref, sb_ref, q_ref, k_ref, v_ref):
    mod = mod_ref[0]
    h = _norm_mod(x_ref[0], ng_ref[0:1], mod[0:1], mod[1:2])
    a = _mm(h, win_ref[...])
    q_pre = _mm(_rms(a[:, :MLA_Q_LORA], qn_ref[...]), wqb_ref[...])
    kv = _mm(_rms(a[:, MLA_Q_LORA:MLA_Q_LORA + MLA_KV_LORA], kvn_ref[...]), wkvb_ref[...])
    c, sa, sb = c_ref[...], sa_ref[...], sb_ref[...]
    k_rope = _rope(a[:, MLA_Q_LORA + MLA_KV_LORA:], c, sa, sb)
    v0 = MLA_HEADS * MLA_NOPE
    for hd in range(MLA_HEADS):
        q0 = hd * MLA_QK_PAD
        q_rope = _rope(q_pre[:, q0 + MLA_NOPE:q0 + MLA_QK_PAD], c, sa, sb)
        q_ref[0, hd] = jnp.concatenate(
            [q_pre[:, q0:q0 + MLA_NOPE], q_rope], axis=-1).astype(BF16)
        k_ref[0, hd] = jnp.concatenate(
            [kv[:, hd * MLA_NOPE:(hd + 1) * MLA_NOPE], k_rope], axis=-1).astype(BF16)
        v_ref[0, hd] = kv[:, v0 + hd * MLA_V:v0 + (hd + 1) * MLA_V].astype(BF16)


def _mla_qkv(x3d, mod, ng, win, qn, kvn, wqb, wkvb, tabs, *, tm):
    b, s, _ = x3d.shape
    per_batch = mod.shape[0] == b
    tab_spec = pl.BlockSpec((tm, LANES), lambda i, t: (t, 0))
    qk_shape = jax.ShapeDtypeStruct((b, MLA_HEADS, s, MLA_QK_PAD), BF16)
    return pl.pallas_call(
        _mla_qkv_kernel,
        grid=(b, s // tm),
        in_specs=[
            pl.BlockSpec((1, tm, D_MODEL), lambda i, t: (i, t, 0)),
            pl.BlockSpec((1, 6, D_MODEL), lambda i, t: (i if per_batch else 0, 0, 0)),
            _const_spec((4, D_MODEL)),
            _const_spec(win.shape),
            _const_spec(qn.shape),
            _const_spec(kvn.shape),
            _const_spec(wqb.shape),
            _const_spec(wkvb.shape),
            tab_spec, tab_spec, tab_spec,
        ],
        out_specs=[
            pl.BlockSpec((1, MLA_HEADS, tm, MLA_QK_PAD), lambda i, t: (i, 0, t, 0)),
            pl.BlockSpec((1, MLA_HEADS, tm, MLA_QK_PAD), lambda i, t: (i, 0, t, 0)),
            pl.BlockSpec((1, MLA_HEADS, tm, MLA_V), lambda i, t: (i, 0, t, 0)),
        ],
        out_shape=[qk_shape, qk_shape, jax.ShapeDtypeStruct((b, MLA_HEADS, s, MLA_V), BF16)],
        compiler_params=_params(2),
        name="mla_qkv",
    )(x3d, mod, ng, win, qn, kvn, wqb, wkvb, *tabs)


def _softmax_chunk(q, k, v, scale):
    s = _mm_nt(q, k)
    m = jnp.max(s, axis=-1, keepdims=True)
    p = jnp.exp((s - m) * scale)
    return m, jnp.sum(p, axis=-1, keepdims=True), jnp.dot(
        p.astype(BF16), v, preferred_element_type=F32)


def _lane_tiles(x):
    return [x[:, t * LANES:(t + 1) * LANES] for t in range(x.shape[1] // LANES)]


def _mla_attn_kernel(q_ref, kc_ref, vc_ref, kl_ref, vl_ref, o_ref,
                     vx_sc, s_0, s_1, p_0, p_1, m_0, m_1, *, tq, kv_chunk, pv_chunk):
    s_sc, p_sc, m_sc = (s_0, s_1), (p_0, p_1), (m_0, m_1)
    lc, l = kc_ref.shape[2], kl_ref.shape[2]
    n_tiles = l // tq
    exp2_scale = MLA_SCALE * LOG2E
    k_chunks = [(kc_ref, c0, min(kv_chunk, lc), c0) for c0 in range(0, lc, kv_chunk)]
    k_chunks += [(kl_ref, c0, kv_chunk, lc + c0) for c0 in range(0, l, kv_chunk)]
    pv_ends = {lc: 0, **{lc + c0 + pv_chunk: lc + c0 for c0 in range(0, l, pv_chunk)}}

    vx_sc[:lc, :MLA_V] = vc_ref[0, 0]
    vx_sc[lc:, :MLA_V] = vl_ref[0, 0]
    vx_sc[:, MLA_V:] = jnp.ones((lc + l, LANES), BF16)

    def rows(tile):
        start = tile * tq
        return pl.ds(start if isinstance(start, int) else pl.multiple_of(start, tq), tq)

    def stage(t, parity):
        do_values = not isinstance(t, int) or 0 <= t - 1 < n_tiles
        do_scores = not isinstance(t, int) or 0 <= t + 1 < n_tiles
        do_exps = not isinstance(t, int) or 0 <= t < n_tiles
        other = 1 - parity
        if do_scores:
            q = q_ref[0, 0, rows(t + 1), :]
        m_part = acc = None
        for k_ref, c0, width, col0 in k_chunks:
            cols = slice(col0, col0 + width)
            if do_exps:
                for t0 in range(cols.start, cols.stop, LANES):
                    lanes = slice(t0, t0 + LANES)
                    p_sc[parity][:, lanes] = jnp.exp2(
                        (s_sc[parity][:, lanes] - m_sc[parity][...]) * exp2_scale).astype(BF16)
            if do_scores:
                s = _mm_nt(q, k_ref[0, 0, c0:c0 + width, :])
                s_sc[other][:, cols] = s
                for s_tile in _lane_tiles(s):
                    m_part = s_tile if m_part is None else jnp.maximum(m_part, s_tile)
            if do_values and cols.stop in pv_ends:
                keys = slice(pv_ends[cols.stop], cols.stop)
                pv = jnp.dot(p_sc[other][:, keys], vx_sc[keys, :], preferred_element_type=F32)
                acc = pv if acc is None else acc + pv
        if do_scores:
            m_sc[other][...] = jnp.broadcast_to(
                jnp.max(m_part, axis=-1, keepdims=True), (tq, LANES))
        if do_values:
            o_ref[0, rows(t - 1), :] = (acc[:, :MLA_V] / acc[:, MLA_V:]).astype(BF16)

    stage(-1, 1)
    stage(0, 0)

    def body(i, carry):
        stage(2 * i + 1, 1)
        stage(2 * i + 2, 0)
        return carry

    lax.fori_loop(0, (n_tiles - 2) // 2, body, 0)
    stage(n_tiles - 1, (n_tiles - 1) % 2)
    stage(n_tiles, n_tiles % 2)


def _mla_attn(q, kc, vc, kl, vl, *, tq, kv_chunk):
    b, h, l, _ = q.shape
    lc = kc.shape[2]
    kern = functools.partial(_mla_attn_kernel, tq=tq, kv_chunk=kv_chunk, pv_chunk=1024)
    return pl.pallas_call(
        kern,
        grid=(b, h),
        in_specs=[
            pl.BlockSpec((1, 1, l, MLA_QK_PAD), lambda i, j: (i, j, 0, 0)),
            pl.BlockSpec((1, 1, lc, MLA_QK_PAD), lambda i, j: (i, j, 0, 0)),
            pl.BlockSpec((1, 1, lc, MLA_V), lambda i, j: (i, j, 0, 0)),
            pl.BlockSpec((1, 1, l, MLA_QK_PAD), lambda i, j: (i, j, 0, 0)),
            pl.BlockSpec((1, 1, l, MLA_V), lambda i, j: (i, j, 0, 0)),
        ],
        out_specs=pl.BlockSpec((1, l, MLA_V), lambda i, j: (i, 0, j)),
        out_shape=jax.ShapeDtypeStruct((b, l, h * MLA_V), BF16),
        scratch_shapes=(
            [pltpu.VMEM((lc + l, 2 * LANES), BF16)] + [pltpu.VMEM((tq, lc + l), F32)] * 2
            + [pltpu.VMEM((tq, lc + l), BF16)] * 2 + [pltpu.VMEM((tq, LANES), F32)] * 2),
        compiler_params=_params(2),
        name="mla_attn",
    )(q, kc, vc, kl, vl)


def _ctx_attn_kernel(q_ref, k_ref, v_ref, o_ref):
    for hd in range(q_ref.shape[1]):
        _, l, acc = _softmax_chunk(q_ref[0, hd], k_ref[0, hd], v_ref[0, hd], MLA_SCALE)
        o_ref[0, :, hd * MLA_V:(hd + 1) * MLA_V] = (acc / l).astype(BF16)


def _ctx_attn(q, k, v):
    b, h, lc, _ = q.shape
    qk_spec = pl.BlockSpec((1, h, lc, MLA_QK_PAD), lambda i: (i, 0, 0, 0))
    return pl.pallas_call(
        _ctx_attn_kernel,
        grid=(b,),
        in_specs=[qk_spec, qk_spec, pl.BlockSpec((1, h, lc, MLA_V), lambda i: (i, 0, 0, 0))],
        out_specs=pl.BlockSpec((1, lc, h * MLA_V), lambda i: (i, 0, 0)),
        out_shape=jax.ShapeDtypeStruct((b, lc, h * MLA_V), BF16),
        compiler_params=_params(1),
        name="mla_ctx_attn",
    )(q, k, v)


def _na_attn_kernel(q_ref, k_ref, v_ref, kc_ref, vc_ref, tab_ref, o_ref,
                    vx_sc, vcx_sc, z_0, z_1, p_0, p_1, m_0, m_1, *, rows):
    z_sc, p_sc, m_sc = (z_0, z_1), (p_0, p_1), (m_0, m_1)
    tq = NA_ROW_BLOCK * GRID_W
    n_loc = NA_WIN_ROWS * GRID_W
    n_blocks = rows // NA_ROW_BLOCK
    exp2_scale = NA_SCALE * LOG2E
    lane_head = lax.broadcasted_iota(jnp.int32, (1, LANES), 1) // NA_HEAD_DIM
    key_row = lax.broadcasted_iota(jnp.int32, (1, n_loc), 1) // GRID_W

    vx_sc[:, :LANES] = v_ref[...]
    vx_sc[:, LANES:] = jnp.ones(v_ref.shape, BF16)
    vcx_sc[:, :LANES] = vc_ref[...]
    vcx_sc[:, LANES:] = jnp.ones(vc_ref.shape, BF16)

    def window(j):
        win0 = jnp.clip(j * NA_ROW_BLOCK - NA_ROWS // 2, 0, rows - NA_WIN_ROWS)
        return win0, pl.ds(pl.multiple_of(win0 * GRID_W, GRID_W), n_loc)

    def q_rows(j):
        start = j * tq
        return pl.ds(start if isinstance(start, int) else pl.multiple_of(start, tq), tq)

    def scores(j, slot):
        win0, key_rows = window(j)
        q = q_ref[q_rows(j), :]
        zero = jnp.zeros_like(q)
        q2 = jnp.concatenate(
            [jnp.where(lane_head == 0, q, zero), jnp.where(lane_head == 1, q, zero)], axis=0)
        s_loc = _mm_nt(q2, k_ref[key_rows, :])
        z_ctx = _mm_nt(q2, kc_ref[...]) * exp2_scale
        z_sc[slot][:, n_loc:] = z_ctx
        m_ctx = jnp.max(z_ctx, axis=-1, keepdims=True)
        for hh in range(2):
            for i in range(NA_ROW_BLOCK):
                r = j * NA_ROW_BLOCK + i
                k_lo = jnp.clip(r - NA_ROWS // 2, 0, rows - NA_ROWS) - win0
                row_mask = jnp.where((key_row >= k_lo) & (key_row < k_lo + NA_ROWS), 0.0, NEG)
                tiles = []
                for kp in range(NA_WIN_ROWS // 2):
                    dr0 = win0 + 2 * kp - r + NA_ROWS - 1
                    tiles.append(tab_ref[hh, jnp.clip(dr0 + 1, 0, 2 * NA_ROWS - 1)])
                band = slice(hh * tq + i * GRID_W, hh * tq + (i + 1) * GRID_W)
                z = s_loc[band, :] * exp2_scale + (jnp.concatenate(tiles, axis=-1) + row_mask)
                z_sc[slot][band, :n_loc] = z
                m_sc[slot][band, :] = jnp.maximum(
                    jnp.max(z, axis=-1, keepdims=True), m_ctx[band, :])

    def exps(slot):
        p_sc[slot][...] = jnp.exp2(z_sc[slot][...] - m_sc[slot][...]).astype(BF16)

    def values(j, slot):
        _, key_rows = window(j)
        acc = (jnp.dot(p_sc[slot][:, :n_loc], vx_sc[key_rows, :], preferred_element_type=F32)
               + jnp.dot(p_sc[slot][:, n_loc:], vcx_sc[...], preferred_element_type=F32))
        out = acc[:, :LANES] / acc[:, LANES:]
        o_ref[q_rows(j), :] = jnp.where(lane_head == 0, out[:tq], out[tq:]).astype(BF16)

    def stage(j, parity):
        if not isinstance(j, int) or 0 <= j - 1 < n_blocks:
            values(j - 1, 1 - parity)
        if not isinstance(j, int) or 0 <= j + 1 < n_blocks:
            scores(j + 1, 1 - parity)
        if not isinstance(j, int) or 0 <= j < n_blocks:
            exps(parity)

    stage(-1, 1)
    stage(0, 0)

    def body(t, carry):
        stage(2 * t + 1, 1)
        stage(2 * t + 2, 0)
        return carry

    lax.fori_loop(0, (n_blocks - 2) // 2, body, 0)
    stage(n_blocks - 1, (n_blocks - 1) % 2)
    stage(n_blocks, n_blocks % 2)


def _na_attn(qkv, kvc, tab, *, batch, seq_len, ctx_len):
    rows = seq_len // GRID_W
    n_pairs = NA_WIDTH // LANES
    tq = NA_ROW_BLOCK * GRID_W
    n_keys = NA_WIN_ROWS * GRID_W + ctx_len
    kern = functools.partial(_na_attn_kernel, rows=rows)
    return pl.pallas_call(
        kern,
        grid=(n_pairs, batch),
        in_specs=[
            pl.BlockSpec((seq_len, LANES), lambda p, b: (b, p)),
            pl.BlockSpec((seq_len, LANES), lambda p, b: (b, n_pairs + p)),
            pl.BlockSpec((seq_len, LANES), lambda p, b: (b, 2 * n_pairs + p)),
            pl.BlockSpec((ctx_len, LANES), lambda p, b: (b, p)),
            pl.BlockSpec((ctx_len, LANES), lambda p, b: (b, n_pairs + p)),
            pl.BlockSpec((2, 2 * NA_ROWS, GRID_W, LANES), lambda p, b: (p, 0, 0, 0)),
        ],
        out_specs=pl.BlockSpec((seq_len, LANES), lambda p, b: (b, p)),
        out_shape=jax.ShapeDtypeStruct((batch * seq_len, NA_WIDTH), BF16),
        scratch_shapes=(
            [pltpu.VMEM((seq_len, 2 * LANES), BF16), pltpu.VMEM((ctx_len, 2 * LANES), BF16)]
            + [pltpu.VMEM((2 * tq, n_keys), F32)] * 2 + [pltpu.VMEM((2 * tq, n_keys), BF16)] * 2
            + [pltpu.VMEM((2 * tq, 1), F32)] * 2),
        compiler_params=_params(2),
        name="na_attn",
    )(qkv, qkv, qkv, kvc, kvc, tab)


def _na_bias_table(rpb):
    cols = np.arange(GRID_W)
    col_start = np.clip(cols - NA_COLS // 2, 0, GRID_W - NA_COLS)
    col_mask = (cols[None, :] >= col_start[:, None]) & (cols[None, :] < col_start[:, None] + NA_COLS)
    dc_idx = np.clip(cols[None, :] - cols[:, None] + NA_COLS - 1, 0, 2 * NA_COLS - 2)
    dc2 = np.concatenate([dc_idx, dc_idx], axis=1)
    mask2 = np.concatenate([col_mask, col_mask], axis=1)
    onehot = (dc2[None] == np.arange(2 * NA_COLS - 1)[:, None, None]).astype(np.float32)
    t = jnp.einsum('hrd,dql->hrql', rpb.astype(F32), onehot, precision=lax.Precision.HIGHEST)
    t = jnp.where(mask2[None, None], t * LOG2E, NEG)
    t = jnp.pad(t, ((0, 0), (1, 1), (0, 0), (0, 0)))
    return jnp.where(np.arange(LANES) < GRID_W, t[:, :-1], t[:, 1:])


def _rope_tables(seq_len):
    t = np.arange(seq_len)
    row = (t // GRID_W).astype(np.float32)
    col = (t % GRID_W).astype(np.float32)
    half = MLA_ROPE // 2
    inv = jnp.asarray(ROPE_BASE, F32) ** (-jnp.arange(0, half, 2, dtype=F32) / half)
    ang = jnp.concatenate([row[:, None] * inv, col[:, None] * inv], axis=-1)
    cos, sin = jnp.cos(ang), jnp.sin(ang)
    z32 = jnp.zeros_like(cos)
    z64 = jnp.zeros((seq_len, 64), F32)
    return (jnp.concatenate([cos, cos, z64], axis=-1),
            jnp.concatenate([z32, sin, z64], axis=-1),
            jnp.concatenate([-sin, z32, z64], axis=-1))


def _identity_rope_tables(seq_len):
    ones = jnp.ones((seq_len, 64), F32)
    z = jnp.zeros((seq_len, 64), F32)
    c = jnp.concatenate([ones, z], axis=-1)
    zz = jnp.zeros((seq_len, LANES), F32)
    return c, zz, zz


def kernel(x, c, ctx, c_ctx, ada_w, ada_b, norm_g, ffn_w_gu, ffn_w_down, pool_w, pool_scale,
           mla_w_in, mla_q_norm, mla_kv_norm, mla_w_qb, mla_w_kvb, mla_w_o, na_w_in, na_rpb,
           na_w_o):
    batch, seq_len, d = x.shape
    ctx_len = ctx.shape[1]
    tm = FFN_TOKEN_TILE

    n_cond = -(-(batch + 1) // 8) * 8
    cond = jnp.zeros((n_cond, d), F32).at[:batch].set(c).at[batch].set(c_ctx)
    mod_all = _adaln(cond, ada_w, ada_b).reshape(DEPTH, n_cond, 6, d)

    wgu = ffn_w_gu.astype(BF16)
    wd = ffn_w_down.astype(BF16)

    xl = x.reshape(batch * seq_len, d)
    xc = ctx.reshape(batch * ctx_len, d)

    for i in range(DEPTH):
        kind = i % N_MIXERS
        j = i // N_MIXERS
        ctx_after = any(l % N_MIXERS != MIX_POOL for l in range(i + 1, DEPTH))
        mod_lat = mod_all[i, :batch]
        mod_ctx = mod_all[i, batch:batch + 1]
        ng = norm_g[i]

        if kind == MIX_POOL:
            pw = pool_w[j].astype(BF16)
            ps = pool_scale[j].reshape(1, d)
            xl = _pool_ffn(xl, mod_lat, ng, pw, ps, wgu, wd, i, seq_len=seq_len, tm=tm)
            if ctx_after:
                xc = _pool_ffn(xc, mod_ctx, ng, pw, ps, wgu, wd, i, seq_len=ctx_len, tm=ctx_len)
        elif kind == MIX_MLA:
            win = jnp.pad(mla_w_in[j], ((0, 0), (0, 64))).astype(BF16)
            wqb = jnp.pad(
                mla_w_qb[j].reshape(MLA_Q_LORA, MLA_HEADS, MLA_NOPE + MLA_ROPE),
                ((0, 0), (0, 0), (0, MLA_QK_PAD - MLA_NOPE - MLA_ROPE)),
            ).reshape(MLA_Q_LORA, MLA_HEADS * MLA_QK_PAD).astype(BF16)
            wkvb = mla_w_kvb[j].reshape(MLA_KV_LORA, MLA_HEADS, 2, MLA_NOPE).transpose(
                0, 2, 1, 3).reshape(MLA_KV_LORA, 2 * MLA_HEADS * MLA_NOPE).astype(BF16)
            qn = mla_q_norm[j].reshape(1, MLA_Q_LORA)
            kvn = mla_kv_norm[j].reshape(1, MLA_KV_LORA)
            wo = mla_w_o[j].astype(BF16)
            ql, kl, vl = _mla_qkv(xl.reshape(batch, seq_len, d), mod_lat, ng, win, qn, kvn, wqb,
                                  wkvb, _rope_tables(seq_len), tm=PROJ_TOKEN_TILE)
            qc, kc, vc = _mla_qkv(xc.reshape(batch, ctx_len, d), mod_ctx, ng, win, qn, kvn, wqb,
                                  wkvb, _identity_rope_tables(ctx_len), tm=ctx_len)
            a_lat = _mla_attn(ql, kc, vc, kl, vl, tq=MLA_Q_TILE, kv_chunk=MLA_KV_CHUNK)
            xl = _proj_ffn(xl, a_lat.reshape(batch * seq_len, -1), mod_lat, ng, wo, wgu, wd, i,
                           tm=tm)
            if ctx_after:
                a_ctx = _ctx_attn(qc, kc, vc)
                xc = _proj_ffn(xc, a_ctx.reshape(batch * ctx_len, -1), mod_ctx, ng, wo, wgu, wd, i,
                               tm=ctx_len)
        else:
            w_in = na_w_in[j].astype(BF16)
            wo = na_w_o[j].astype(BF16)
            qkv = _norm_proj(xl, mod_lat, ng, w_in, tm=PROJ_TOKEN_TILE)
            kvc = _norm_proj(xc, mod_ctx, ng, w_in[:, NA_WIDTH:], tm=ctx_len)
            a_lat = _na_attn(qkv, kvc, _na_bias_table(na_rpb[j]), batch=batch, seq_len=seq_len,
                             ctx_len=ctx_len)
            xl = _proj_ffn(xl, a_lat, mod_lat, ng, wo, wgu, wd, i, tm=tm)
            if ctx_after:
                raise NotImplementedError("context output of a neighbourhood layer")
    return xl.reshape(batch, seq_len, d)
```

```python
import functools

import jax
import jax.numpy as jnp
import numpy as np
from jax import lax
from jax.experimental import pallas as pl
from jax.experimental.pallas import tpu as pltpu

D_MODEL = 1024
DEPTH = 4
GRID_W = 64
N_MIXERS = 3
MIX_POOL, MIX_MLA, MIX_NA = 0, 1, 2
RMS_EPS = 1e-6

POOL_WINDOWS = (2, 4, 8, 16)
POOL_GROUP_DIM = D_MODEL // len(POOL_WINDOWS)
POOL_HALO = 8

MLA_HEADS = 8
MLA_NOPE = 128
MLA_ROPE = 64
MLA_V = 128
MLA_Q_LORA = 384
MLA_KV_LORA = 256
MLA_QK_PAD = 256
MLA_SCALE = (MLA_NOPE + MLA_ROPE) ** -0.5
ROPE_BASE = 10000.0

NA_HEADS = 16
NA_HEAD_DIM = 64
NA_WIDTH = NA_HEADS * NA_HEAD_DIM
NA_ROWS = 8
NA_COLS = 16
NA_SCALE = NA_HEAD_DIM ** -0.5
NA_ROW_BLOCK = 4
NA_WIN_ROWS = NA_ROW_BLOCK + NA_ROWS

FFN_HIDDEN = 2816
FFN_CHUNK = 256

FFN_TOKEN_TILE = 512
PROJ_TOKEN_TILE = 1024
MLA_Q_TILE = 512
MLA_KV_CHUNK = 256
FFN_NORM_GROUPS = 8

LANES = 128
VMEM_LIMIT_BYTES = 56 * 1024 * 1024

NEG = -1e30
LOG2E = float(np.log2(np.e))

BF16 = jnp.bfloat16
F32 = jnp.float32


def _params(n_axes, flags=None):
    return pltpu.CompilerParams(
        dimension_semantics=("arbitrary",) * n_axes, vmem_limit_bytes=VMEM_LIMIT_BYTES,
        flags=flags)


def _const_spec(shape):
    nd = len(shape)
    return pl.BlockSpec(shape, lambda *_: (0,) * nd, pipeline_mode=pl.Buffered(1))


def _layer_spec(stacked_shape, layer):
    rest = tuple(stacked_shape[1:])
    return pl.BlockSpec((None,) + rest, lambda *_: (layer,) + (0,) * len(rest),
                        pipeline_mode=pl.Buffered(1))


def _rms(xf, g):
    return xf * lax.rsqrt(jnp.mean(xf * xf, axis=-1, keepdims=True) + RMS_EPS) * g


def _norm_mod(xf, g, shift, scale):
    return _rms(xf, g * (1.0 + scale)) + shift


def _mm(a, b):
    return jnp.dot(a.astype(BF16), b, preferred_element_type=F32)


def _mm_nt(a, b):
    return lax.dot_general(a, b, (((1,), (1,)), ((), ())), preferred_element_type=F32)


def _silu(x):
    return x / (1.0 + jnp.exp(-x))


def _adaln_kernel(cond_ref, w_ref, b_ref, o_ref):
    cond = cond_ref[...]
    o_ref[0] = _mm(_silu(cond), w_ref[0].astype(BF16)) + b_ref[0]


def _adaln(cond, ada_w, ada_b):
    rows = cond.shape[0]
    tn = 1024
    n_out = ada_w.shape[-1]
    return pl.pallas_call(
        _adaln_kernel,
        grid=(DEPTH, n_out // tn),
        in_specs=[
            pl.BlockSpec((rows, D_MODEL), lambda l, n: (0, 0)),
            pl.BlockSpec((1, D_MODEL, tn), lambda l, n: (l, 0, n)),
            pl.BlockSpec((1, 1, tn), lambda l, n: (l, 0, n)),
        ],
        out_specs=pl.BlockSpec((1, rows, tn), lambda l, n: (l, 0, n)),
        out_shape=jax.ShapeDtypeStruct((DEPTH, rows, n_out), F32),
        compiler_params=_params(2),
        name="adaln",
    )(cond, ada_w, ada_b.reshape(DEPTH, 1, n_out))


def _pool_ffn_kernel(x_ref, xp_ref, xn_ref, modc_ref, modp_ref, ng_ref, pw_ref, ps_ref, wgu_ref,
                     wd_ref, o_ref, hbuf, x1_0, x1_1, h2_0, h2_1, abuf, *, tm, seq_len, n_tiles):
    tiles_per_seq = seq_len // tm
    tile = jnp.minimum(pl.program_id(0), n_tiles - 1)
    p0 = (tile % tiles_per_seq) * tm

    def front(mod):
        def hfun(xv):
            return _norm_mod(xv, ng_ref[0:1], mod[0:1], mod[1:2])

        hbuf[0:POOL_HALO, :] = jnp.where(p0 > 0, hfun(xp_ref[...]), 0.0)
        hbuf[POOL_HALO:POOL_HALO + tm, :] = hfun(x_ref[...])
        hbuf[POOL_HALO + tm:, :] = jnp.where(p0 + tm < seq_len, hfun(xn_ref[...]), 0.0)

        pos = p0 + lax.broadcasted_iota(jnp.int32, (tm, 1), 0)
        ys = []
        for g, win in enumerate(POOL_WINDOWS):
            cols = slice(g * POOL_GROUP_DIM, (g + 1) * POOL_GROUP_DIM)
            before, after = win // 2, win - win // 2
            s = None
            for d in range(-before, after):
                term = hbuf[POOL_HALO + d:POOL_HALO + d + tm, cols]
                s = term if s is None else s + term
            lo = jnp.maximum(pos - before, 0)
            hi = jnp.minimum(pos + after, seq_len)
            inv_cnt = 1.0 / (hi - lo).astype(F32)
            pooled = s * inv_cnt - hbuf[POOL_HALO:POOL_HALO + tm, cols]
            ys.append(_mm(pooled, pw_ref[g]))
        return jnp.concatenate(ys, axis=-1) * ps_ref[...]

    _staggered_ffn(front, x_ref, modc_ref, modp_ref, ng_ref, wgu_ref, wd_ref, o_ref,
                   (x1_0, x1_1), (h2_0, h2_1), abuf)


def _pool_ffn(x2d, mod, ng, pw, ps, wgu, wd, layer, *, seq_len, tm):
    t_total = x2d.shape[0]
    n_tiles = t_total // tm
    halo_blocks = t_total // POOL_HALO
    per_tile = tm // POOL_HALO
    cur, prev, mod_specs, weight_specs, scratch = _staggered_specs(
        tm, n_tiles, mod, wgu, wd, layer)
    kern = functools.partial(_pool_ffn_kernel, tm=tm, seq_len=seq_len, n_tiles=n_tiles)
    return pl.pallas_call(
        kern,
        grid=(n_tiles + 1,),
        in_specs=[
            pl.BlockSpec((tm, D_MODEL), lambda t: (cur(t), 0)),
            pl.BlockSpec((POOL_HALO, D_MODEL),
                         lambda t: (jnp.maximum(cur(t) * per_tile - 1, 0), 0)),
            pl.BlockSpec((POOL_HALO, D_MODEL),
                         lambda t: (jnp.minimum((cur(t) + 1) * per_tile, halo_blocks - 1), 0)),
            *mod_specs,
            _const_spec((4, D_MODEL)),
            _const_spec(pw.shape),
            _const_spec((1, D_MODEL)),
            *weight_specs,
        ],
        out_specs=pl.BlockSpec((tm, D_MODEL), lambda t: (prev(t), 0)),
        out_shape=jax.ShapeDtypeStruct(x2d.shape, F32),
        scratch_shapes=[pltpu.VMEM((tm + 2 * POOL_HALO, D_MODEL), F32)] + scratch,
        compiler_params=_params(1),
        name="pool_ffn",
    )(x2d, x2d, x2d, mod, mod, ng, pw, ps, wgu, wd)


def _zero_after(v):
    bits = pltpu.bitcast(v, jnp.int32)
    return lax.shift_right_logical(lax.shift_right_logical(bits, 16), 16).astype(F32)


def _ffn_tail(x1_ref, h2_ref, mod, ng_ref, wgu_ref, wd_ref, abuf, anchors):
    for c in range(FFN_HIDDEN // FFN_CHUNK):
        lo = c * FFN_CHUNK
        g = jnp.dot(h2_ref[...], wgu_ref[:, lo:lo + FFN_CHUNK], preferred_element_type=F32)
        u = jnp.dot(h2_ref[...], wgu_ref[:, FFN_HIDDEN + lo:FFN_HIDDEN + lo + FFN_CHUNK],
                    preferred_element_type=F32)
        if 1 <= c <= len(anchors):
            rows = anchors[c - 1].shape[0]
            u = jnp.concatenate([u[:rows] + anchors[c - 1], u[rows:]], axis=0)
        abuf[:, lo:lo + FFN_CHUNK] = (_silu(g) * u).astype(BF16)
    f = jnp.dot(abuf[...], wd_ref[...], preferred_element_type=F32)
    return x1_ref[...] + _rms(f, ng_ref[3:4] * mod[5:6])


def _staggered_ffn(front, x_ref, modc_ref, modp_ref, ng_ref, wgu_ref, wd_ref, o_ref,
                   x1_sc, h2_sc, abuf):
    t = pl.program_id(0)
    last = pl.num_programs(0) - 1
    tm = x_ref.shape[0]

    def start(cur):
        mod = modc_ref[0]
        y = front(mod)
        group = tm // FFN_NORM_GROUPS
        anchors = []
        for r0 in range(0, tm, group):
            rg = slice(r0, r0 + group)
            x1 = x_ref[rg, :] + _rms(y[rg], ng_ref[1:2] * mod[2:3])
            x1_sc[cur][rg, :] = x1
            h2 = _norm_mod(x1, ng_ref[2:3], mod[3:4], mod[4:5])
            h2_sc[cur][rg, :] = h2.astype(BF16)
            anchors.append(_zero_after(h2[:8, :FFN_CHUNK]))
        return anchors

    def finish(prev, anchors):
        o_ref[...] = _ffn_tail(x1_sc[prev], h2_sc[prev], modp_ref[0], ng_ref, wgu_ref, wd_ref, abuf,
                               anchors)

    @pl.when(t == 0)
    def _():
        start(0)

    for parity in (0, 1):
        @pl.when((t > 0) & (t < last) & (t % 2 == parity))
        def _(parity=parity):
            finish(1 - parity, start(parity))

        @pl.when((t == last) & (t % 2 == parity))
        def _(parity=parity):
            finish(1 - parity, [])


def _staggered_specs(tm, n_tiles, mod, wgu, wd, layer):
    tiles_per_mod = n_tiles // mod.shape[0]

    def cur(t):
        return jnp.minimum(t, n_tiles - 1)

    def prev(t):
        return jnp.maximum(t - 1, 0)

    mod_specs = [
        pl.BlockSpec((1, 6, D_MODEL), lambda t: (cur(t) // tiles_per_mod, 0, 0)),
        pl.BlockSpec((1, 6, D_MODEL), lambda t: (prev(t) // tiles_per_mod, 0, 0)),
    ]
    weight_specs = [_layer_spec(wgu.shape, layer), _layer_spec(wd.shape, layer)]
    scratch = ([pltpu.VMEM((tm, D_MODEL), F32)] * 2 + [pltpu.VMEM((tm, D_MODEL), BF16)] * 2
               + [pltpu.VMEM((tm, FFN_HIDDEN), BF16)])
    return cur, prev, mod_specs, weight_specs, scratch


def _proj_ffn_kernel(x_ref, a_ref, modc_ref, modp_ref, ng_ref, wo_ref, wgu_ref, wd_ref, o_ref,
                     x1_0, x1_1, h2_0, h2_1, abuf):
    def front(mod):
        del mod
        return jnp.dot(a_ref[...], wo_ref[...], preferred_element_type=F32)

    _staggered_ffn(front, x_ref, modc_ref, modp_ref, ng_ref, wgu_ref, wd_ref, o_ref,
                   (x1_0, x1_1), (h2_0, h2_1), abuf)


def _proj_ffn(x2d, attn2d, mod, ng, wo, wgu, wd, layer, *, tm):
    n_tiles = x2d.shape[0] // tm
    cur, prev, mod_specs, weight_specs, scratch = _staggered_specs(
        tm, n_tiles, mod, wgu, wd, layer)
    return pl.pallas_call(
        _proj_ffn_kernel,
        grid=(n_tiles + 1,),
        in_specs=[
            pl.BlockSpec((tm, D_MODEL), lambda t: (cur(t), 0)),
            pl.BlockSpec((tm, attn2d.shape[1]), lambda t: (cur(t), 0)),
            *mod_specs,
            _const_spec((4, D_MODEL)),
            _const_spec(wo.shape),
            *weight_specs,
        ],
        out_specs=pl.BlockSpec((tm, D_MODEL), lambda t: (prev(t), 0)),
        out_shape=jax.ShapeDtypeStruct(x2d.shape, F32),
        scratch_shapes=scratch,
        compiler_params=_params(1),
        name="proj_ffn",
    )(x2d, attn2d, mod, mod, ng, wo, wgu, wd)


def _norm_proj_kernel(x_ref, mod_ref, ng_ref, w_ref, o_ref, *, n_chunks, chunk):
    mod = mod_ref[0]
    h = _norm_mod(x_ref[...], ng_ref[0:1], mod[0:1], mod[1:2]).astype(BF16)
    for c in range(n_chunks):
        sl = slice(c * chunk, (c + 1) * chunk)
        o_ref[:, sl] = jnp.dot(h, w_ref[:, sl], preferred_element_type=F32).astype(BF16)


def _norm_proj(x2d, mod, ng, w, *, tm):
    n_tiles = x2d.shape[0] // tm
    tiles_per_mod = n_tiles // mod.shape[0]
    n_out = w.shape[1]
    chunk = 1024
    kern = functools.partial(_norm_proj_kernel, n_chunks=n_out // chunk, chunk=chunk)
    return pl.pallas_call(
        kern,
        grid=(n_tiles,),
        in_specs=[
            pl.BlockSpec((tm, D_MODEL), lambda t: (t, 0)),
            pl.BlockSpec((1, 6, D_MODEL), lambda t: (t // tiles_per_mod, 0, 0)),
            _const_spec((4, D_MODEL)),
            _const_spec(w.shape),
        ],
        out_specs=pl.BlockSpec((tm, n_out), lambda t: (t, 0)),
        out_shape=jax.ShapeDtypeStruct((x2d.shape[0], n_out), BF16),
        compiler_params=_params(1),
        name="norm_proj",
    )(x2d, mod, ng, w)


def _rope(x, c, sa, sb):
    return x * c + pltpu.roll(x, 32, 1) * sa + pltpu.roll(x, 96, 1) * sb


def _mla_qkv_kernel(x_ref, mod_ref, ng_ref, win_ref, qn_ref, kvn_ref, wqb_ref, wkvb_ref,
                    c_ref, sa_ref, sb_ref, q_ref, k_ref, v_ref):
    mod = mod_ref[0]
    h = _norm_mod(x_ref[0], ng_ref[0:1], mod[0:1], mod[1:2])
    a = _mm(h, win_ref[...])
    q_pre = _mm(_rms(a[:, :MLA_Q_LORA], qn_ref[...]), wqb_ref[...])
    kv = _mm(_rms(a[:, MLA_Q_LORA:MLA_Q_LORA + MLA_KV_LORA], kvn_ref[...]), wkvb_ref[...])
    c, sa, sb = c_ref[...], sa_ref[...], sb_ref[...]
    k_rope = _rope(a[:, MLA_Q_LORA + MLA_KV_LORA:], c, sa, sb)
    v0 = MLA_HEADS * MLA_NOPE
    for hd in range(MLA_HEADS):
        q0 = hd * MLA_QK_PAD
        q_rope = _rope(q_pre[:, q0 + MLA_NOPE:q0 + MLA_QK_PAD], c, sa, sb)
        q_ref[0, hd] = jnp.concatenate(
            [q_pre[:, q0:q0 + MLA_NOPE], q_rope], axis=-1).astype(BF16)
        k_ref[0, hd] = jnp.concatenate(
            [kv[:, hd * MLA_NOPE:(hd + 1) * MLA_NOPE], k_rope], axis=-1).astype(BF16)
        v_ref[0, hd] = kv[:, v0 + hd * MLA_V:v0 + (hd + 1) * MLA_V].astype(BF16)


def _mla_qkv(x3d, mod, ng, win, qn, kvn, wqb, wkvb, tabs, *, tm):
    b, s, _ = x3d.shape
    per_batch = mod.shape[0] == b
    tab_spec = pl.BlockSpec((tm, LANES), lambda i, t: (t, 0))
    qk_shape = jax.ShapeDtypeStruct((b, MLA_HEADS, s, MLA_QK_PAD), BF16)
    return pl.pallas_call(
        _mla_qkv_kernel,
        grid=(b, s // tm),
        in_specs=[
            pl.BlockSpec((1, tm, D_MODEL), lambda i, t: (i, t, 0)),
            pl.BlockSpec((1, 6, D_MODEL), lambda i, t: (i if per_batch else 0, 0, 0)),
            _const_spec((4, D_MODEL)),
            _const_spec(win.shape),
            _const_spec(qn.shape),
            _const_spec(kvn.shape),
            _const_spec(wqb.shape),
            _const_spec(wkvb.shape),
            tab_spec, tab_spec, tab_spec,
        ],
        out_specs=[
            pl.BlockSpec((1, MLA_HEADS, tm, MLA_QK_PAD), lambda i, t: (i, 0, t, 0)),
            pl.BlockSpec((1, MLA_HEADS, tm, MLA_QK_PAD), lambda i, t: (i, 0, t, 0)),
            pl.BlockSpec((1, MLA_HEADS, tm, MLA_V), lambda i, t: (i, 0, t, 0)),
        ],
        out_shape=[qk_shape, qk_shape, jax.ShapeDtypeStruct((b, MLA_HEADS, s, MLA_V), BF16)],
        compiler_params=_params(2),
        name="mla_qkv",
    )(x3d, mod, ng, win, qn, kvn, wqb, wkvb, *tabs)


def _softmax_chunk(q, k, v, scale):
    s = _mm_nt(q, k)
    m = jnp.max(s, axis=-1, keepdims=True)
    p = jnp.exp((s - m) * scale)
    return m, jnp.sum(p, axis=-1, keepdims=True), jnp.dot(
        p.astype(BF16), v, preferred_element_type=F32)


def _lane_tiles(x):
    return [x[:, t * LANES:(t + 1) * LANES] for t in range(x.shape[1] // LANES)]


def _mla_attn_kernel(q_ref, kc_ref, vc_ref, kl_ref, vl_ref, o_ref,
                     vx_sc, s_0, s_1, p_0, p_1, m_0, m_1, *, tq, kv_chunk, pv_chunk):
    s_sc, p_sc, m_sc = (s_0, s_1), (p_0, p_1), (m_0, m_1)
    lc, l = kc_ref.shape[2], kl_ref.shape[2]
    n_tiles = l // tq
    exp2_scale = MLA_SCALE * LOG2E
    k_chunks = [(kc_ref, c0, min(kv_chunk, lc), c0) for c0 in range(0, lc, kv_chunk)]
    k_chunks += [(kl_ref, c0, kv_chunk, lc + c0) for c0 in range(0, l, kv_chunk)]
    pv_ends = {lc: 0, **{lc + c0 + pv_chunk: lc + c0 for c0 in range(0, l, pv_chunk)}}

    vx_sc[:lc, :MLA_V] = vc_ref[0, 0]
    vx_sc[lc:, :MLA_V] = vl_ref[0, 0]
    vx_sc[:, MLA_V:] = jnp.ones((lc + l, LANES), BF16)

    def rows(tile):
        start = tile * tq
        return pl.ds(start if isinstance(start, int) else pl.multiple_of(start, tq), tq)

    def stage(t, parity):
        do_values = not isinstance(t, int) or 0 <= t - 1 < n_tiles
        do_scores = not isinstance(t, int) or 0 <= t + 1 < n_tiles
        do_exps = not isinstance(t, int) or 0 <= t < n_tiles
        other = 1 - parity
        if do_scores:
            q = q_ref[0, 0, rows(t + 1), :]
        m_part = acc = None
        for k_ref, c0, width, col0 in k_chunks:
            cols = slice(col0, col0 + width)
            if do_exps:
                for t0 in range(cols.start, cols.stop, LANES):
                    lanes = slice(t0, t0 + LANES)
                    p_sc[parity][:, lanes] = jnp.exp2(
                        (s_sc[parity][:, lanes] - m_sc[parity][...]) * exp2_scale).astype(BF16)
            if do_scores:
                s = _mm_nt(q, k_ref[0, 0, c0:c0 + width, :])
                s_sc[other][:, cols] = s
                for s_tile in _lane_tiles(s):
                    m_part = s_tile if m_part is None else jnp.maximum(m_part, s_tile)
            if do_values and cols.stop in pv_ends:
                keys = slice(pv_ends[cols.stop], cols.stop)
                pv = jnp.dot(p_sc[other][:, keys], vx_sc[keys, :], preferred_element_type=F32)
                acc = pv if acc is None else acc + pv
        if do_scores:
            m_sc[other][...] = jnp.broadcast_to(
                jnp.max(m_part, axis=-1, keepdims=True), (tq, LANES))
        if do_values:
            o_ref[0, rows(t - 1), :] = (acc[:, :MLA_V] / acc[:, MLA_V:]).astype(BF16)

    stage(-1, 1)
    stage(0, 0)

    def body(i, carry):
        stage(2 * i + 1, 1)
        stage(2 * i + 2, 0)
        return carry

    lax.fori_loop(0, (n_tiles - 2) // 2, body, 0)
    stage(n_tiles - 1, (n_tiles - 1) % 2)
    stage(n_tiles, n_tiles % 2)


def _mla_attn(q, kc, vc, kl, vl, *, tq, kv_chunk):
    b, h, l, _ = q.shape
    lc = kc.shape[2]
    kern = functools.partial(_mla_attn_kernel, tq=tq, kv_chunk=kv_chunk, pv_chunk=1024)
    return pl.pallas_call(
        kern,
        grid=(b, h),
        in_specs=[
            pl.BlockSpec((1, 1, l, MLA_QK_PAD), lambda i, j: (i, j, 0, 0)),
            pl.BlockSpec((1, 1, lc, MLA_QK_PAD), lambda i, j: (i, j, 0, 0)),
            pl.BlockSpec((1, 1, lc, MLA_V), lambda i, j: (i, j, 0, 0)),
            pl.BlockSpec((1, 1, l, MLA_QK_PAD), lambda i, j: (i, j, 0, 0)),
            pl.BlockSpec((1, 1, l, MLA_V), lambda i, j: (i, j, 0, 0)),
        ],
        out_specs=pl.BlockSpec((1, l, MLA_V), lambda i, j: (i, 0, j)),
        out_shape=jax.ShapeDtypeStruct((b, l, h * MLA_V), BF16),
        scratch_shapes=(
            [pltpu.VMEM((lc + l, 2 * LANES), BF16)] + [pltpu.VMEM((tq, lc + l), F32)] * 2
            + [pltpu.VMEM((tq, lc + l), BF16)] * 2 + [pltpu.VMEM((tq, LANES), F32)] * 2),
        compiler_params=_params(2),
        name="mla_attn",
    )(q, kc, vc, kl, vl)


def _ctx_attn_kernel(q_ref, k_ref, v_ref, o_ref):
    for hd in range(q_ref.shape[1]):
        _, l, acc = _softmax_chunk(q_ref[0, hd], k_ref[0, hd], v_ref[0, hd], MLA_SCALE)
        o_ref[0, :, hd * MLA_V:(hd + 1) * MLA_V] = (acc / l).astype(BF16)


def _ctx_attn(q, k, v):
    b, h, lc, _ = q.shape
    qk_spec = pl.BlockSpec((1, h, lc, MLA_QK_PAD), lambda i: (i, 0, 0, 0))
    return pl.pallas_call(
        _ctx_attn_kernel,
        grid=(b,),
        in_specs=[qk_spec, qk_spec, pl.BlockSpec((1, h, lc, MLA_V), lambda i: (i, 0, 0, 0))],
        out_specs=pl.BlockSpec((1, lc, h * MLA_V), lambda i: (i, 0, 0)),
        out_shape=jax.ShapeDtypeStruct((b, lc, h * MLA_V), BF16),
        compiler_params=_params(1),
        name="mla_ctx_attn",
    )(q, k, v)


def _na_attn_kernel(q_ref, k_ref, v_ref, kc_ref, vc_ref, tab_ref, o_ref,
                    vx_sc, vcx_sc, z_0, z_1, p_0, p_1, m_0, m_1, *, rows):
    z_sc, p_sc, m_sc = (z_0, z_1), (p_0, p_1), (m_0, m_1)
    tq = NA_ROW_BLOCK * GRID_W
    n_loc = NA_WIN_ROWS * GRID_W
    n_blocks = rows // NA_ROW_BLOCK
    exp2_scale = NA_SCALE * LOG2E
    lane_head = lax.broadcasted_iota(jnp.int32, (1, LANES), 1) // NA_HEAD_DIM
    key_row = lax.broadcasted_iota(jnp.int32, (1, n_loc), 1) // GRID_W

    vx_sc[:, :LANES] = v_ref[...]
    vx_sc[:, LANES:] = jnp.ones(v_ref.shape, BF16)
    vcx_sc[:, :LANES] = vc_ref[...]
    vcx_sc[:, LANES:] = jnp.ones(vc_ref.shape, BF16)

    def window(j):
        win0 = jnp.clip(j * NA_ROW_BLOCK - NA_ROWS // 2, 0, rows - NA_WIN_ROWS)
        return win0, pl.ds(pl.multiple_of(win0 * GRID_W, GRID_W), n_loc)

    def q_rows(j):
        start = j * tq
        return pl.ds(start if isinstance(start, int) else pl.multiple_of(start, tq), tq)

    def scores(j, slot):
        win0, key_rows = window(j)
        q = q_ref[q_rows(j), :]
        zero = jnp.zeros_like(q)
        q2 = jnp.concatenate(
            [jnp.where(lane_head == 0, q, zero), jnp.where(lane_head == 1, q, zero)], axis=0)
        s_loc = _mm_nt(q2, k_ref[key_rows, :])
        z_ctx = _mm_nt(q2, kc_ref[...]) * exp2_scale
        z_sc[slot][:, n_loc:] = z_ctx
        m_ctx = jnp.max(z_ctx, axis=-1, keepdims=True)
        for hh in range(2):
            for i in range(NA_ROW_BLOCK):
                r = j * NA_ROW_BLOCK + i
                k_lo = jnp.clip(r - NA_ROWS // 2, 0, rows - NA_ROWS) - win0
                row_mask = jnp.where((key_row >= k_lo) & (key_row < k_lo + NA_ROWS), 0.0, NEG)
                tiles = []
                for kp in range(NA_WIN_ROWS // 2):
                    dr0 = win0 + 2 * kp - r + NA_ROWS - 1
                    tiles.append(tab_ref[hh, jnp.clip(dr0 + 1, 0, 2 * NA_ROWS - 1)])
                band = slice(hh * tq + i * GRID_W, hh * tq + (i + 1) * GRID_W)
                z = s_loc[band, :] * exp2_scale + (jnp.concatenate(tiles, axis=-1) + row_mask)
                z_sc[slot][band, :n_loc] = z
                m_sc[slot][band, :] = jnp.maximum(
                    jnp.max(z, axis=-1, keepdims=True), m_ctx[band, :])

    def exps(slot):
        p_sc[slot][...] = jnp.exp2(z_sc[slot][...] - m_sc[slot][...]).astype(BF16)

    def values(j, slot):
        _, key_rows = window(j)
        acc = (jnp.dot(p_sc[slot][:, :n_loc], vx_sc[key_rows, :], preferred_element_type=F32)
               + jnp.dot(p_sc[slot][:, n_loc:], vcx_sc[...], preferred_element_type=F32))
        out = acc[:, :LANES] / acc[:, LANES:]
        o_ref[q_rows(j), :] = jnp.where(lane_head == 0, out[:tq], out[tq:]).astype(BF16)

    def stage(j, parity):
        if not isinstance(j, int) or 0 <= j - 1 < n_blocks:
            values(j - 1, 1 - parity)
        if not isinstance(j, int) or 0 <= j + 1 < n_blocks:
            scores(j + 1, 1 - parity)
        if not isinstance(j, int) or 0 <= j < n_blocks:
            exps(parity)

    stage(-1, 1)
    stage(0, 0)

    def body(t, carry):
        stage(2 * t + 1, 1)
        stage(2 * t + 2, 0)
        return carry

    lax.fori_loop(0, (n_blocks - 2) // 2, body, 0)
    stage(n_blocks - 1, (n_blocks - 1) % 2)
    stage(n_blocks, n_blocks % 2)


def _na_attn(qkv, kvc, tab, *, batch, seq_len, ctx_len):
    rows = seq_len // GRID_W
    n_pairs = NA_WIDTH // LANES
    tq = NA_ROW_BLOCK * GRID_W
    n_keys = NA_WIN_ROWS * GRID_W + ctx_len
    kern = functools.partial(_na_attn_kernel, rows=rows)
    return pl.pallas_call(
        kern,
        grid=(n_pairs, batch),
        in_specs=[
            pl.BlockSpec((seq_len, LANES), lambda p, b: (b, p)),
            pl.BlockSpec((seq_len, LANES), lambda p, b: (b, n_pairs + p)),
            pl.BlockSpec((seq_len, LANES), lambda p, b: (b, 2 * n_pairs + p)),
            pl.BlockSpec((ctx_len, LANES), lambda p, b: (b, p)),
            pl.BlockSpec((ctx_len, LANES), lambda p, b: (b, n_pairs + p)),
            pl.BlockSpec((2, 2 * NA_ROWS, GRID_W, LANES), lambda p, b: (p, 0, 0, 0)),
        ],
        out_specs=pl.BlockSpec((seq_len, LANES), lambda p, b: (b, p)),
        out_shape=jax.ShapeDtypeStruct((batch * seq_len, NA_WIDTH), BF16),
        scratch_shapes=(
            [pltpu.VMEM((seq_len, 2 * LANES), BF16), pltpu.VMEM((ctx_len, 2 * LANES), BF16)]
            + [pltpu.VMEM((2 * tq, n_keys), F32)] * 2 + [pltpu.VMEM((2 * tq, n_keys), BF16)] * 2
            + [pltpu.VMEM((2 * tq, 1), F32)] * 2),
        compiler_params=_params(2),
        name="na_attn",
    )(qkv, qkv, qkv, kvc, kvc, tab)


def _na_bias_table(rpb):
    cols = np.arange(GRID_W)
    col_start = np.clip(cols - NA_COLS // 2, 0, GRID_W - NA_COLS)
    col_mask = (cols[None, :] >= col_start[:, None]) & (cols[None, :] < col_start[:, None] + NA_COLS)
    dc_idx = np.clip(cols[None, :] - cols[:, None] + NA_COLS - 1, 0, 2 * NA_COLS - 2)
    t = jnp.where(col_mask[None, None], rpb.astype(F32)[:, :, dc_idx] * LOG2E, NEG)
    zero = jnp.zeros_like(t[:, :1])
    left = jnp.concatenate([zero, t], axis=1)
    right = jnp.concatenate([t, zero], axis=1)
    return jnp.concatenate([left, right], axis=-1)


def _rope_tables(seq_len):
    t = np.arange(seq_len)
    row = (t // GRID_W).astype(np.float32)
    col = (t % GRID_W).astype(np.float32)
    half = MLA_ROPE // 2
    inv = jnp.asarray(ROPE_BASE, F32) ** (-jnp.arange(0, half, 2, dtype=F32) / half)
    ang = jnp.concatenate([row[:, None] * inv, col[:, None] * inv], axis=-1)
    cos, sin = jnp.cos(ang), jnp.sin(ang)
    z32 = jnp.zeros_like(cos)
    z64 = jnp.zeros((seq_len, 64), F32)
    return (jnp.concatenate([cos, cos, z64], axis=-1),
            jnp.concatenate([z32, sin, z64], axis=-1),
            jnp.concatenate([-sin, z32, z64], axis=-1))


def _identity_rope_tables(seq_len):
    ones = jnp.ones((seq_len, 64), F32)
    z = jnp.zeros((seq_len, 64), F32)
    c = jnp.concatenate([ones, z], axis=-1)
    zz = jnp.zeros((seq_len, LANES), F32)
    return c, zz, zz


def kernel(x, c, ctx, c_ctx, ada_w, ada_b, norm_g, ffn_w_gu, ffn_w_down, pool_w, pool_scale,
           mla_w_in, mla_q_norm, mla_kv_norm, mla_w_qb, mla_w_kvb, mla_w_o, na_w_in, na_rpb,
           na_w_o):
    batch, seq_len, d = x.shape
    ctx_len = ctx.shape[1]
    tm = FFN_TOKEN_TILE

    n_cond = -(-(batch + 1) // 8) * 8
    cond = jnp.zeros((n_cond, d), F32).at[:batch].set(c).at[batch].set(c_ctx)
    mod_all = _adaln(cond, ada_w, ada_b).reshape(DEPTH, n_cond, 6, d)

    wgu = ffn_w_gu.astype(BF16)
    wd = ffn_w_down.astype(BF16)

    xl = x.reshape(batch * seq_len, d)
    xc = ctx.reshape(batch * ctx_len, d)

    for i in range(DEPTH):
        kind = i % N_MIXERS
        j = i // N_MIXERS
        ctx_after = any(l % N_MIXERS != MIX_POOL for l in range(i + 1, DEPTH))
        mod_lat = mod_all[i, :batch]
        mod_ctx = mod_all[i, batch:batch + 1]
        ng = norm_g[i]

        if kind == MIX_POOL:
            pw = pool_w[j].astype(BF16)
            ps = pool_scale[j].reshape(1, d)
            xl = _pool_ffn(xl, mod_lat, ng, pw, ps, wgu, wd, i, seq_len=seq_len, tm=tm)
            if ctx_after:
                xc = _pool_ffn(xc, mod_ctx, ng, pw, ps, wgu, wd, i, seq_len=ctx_len, tm=ctx_len)
        elif kind == MIX_MLA:
            win = jnp.pad(mla_w_in[j], ((0, 0), (0, 64))).astype(BF16)
            wqb = jnp.pad(
                mla_w_qb[j].reshape(MLA_Q_LORA, MLA_HEADS, MLA_NOPE + MLA_ROPE),
                ((0, 0), (0, 0), (0, MLA_QK_PAD - MLA_NOPE - MLA_ROPE)),
            ).reshape(MLA_Q_LORA, MLA_HEADS * MLA_QK_PAD).astype(BF16)
            wkvb = mla_w_kvb[j].reshape(MLA_KV_LORA, MLA_HEADS, 2, MLA_NOPE).transpose(
                0, 2, 1, 3).reshape(MLA_KV_LORA, 2 * MLA_HEADS * MLA_NOPE).astype(BF16)
            qn = mla_q_norm[j].reshape(1, MLA_Q_LORA)
            kvn = mla_kv_norm[j].reshape(1, MLA_KV_LORA)
            wo = mla_w_o[j].astype(BF16)
            ql, kl, vl = _mla_qkv(xl.reshape(batch, seq_len, d), mod_lat, ng, win, qn, kvn, wqb,
                                  wkvb, _rope_tables(seq_len), tm=PROJ_TOKEN_TILE)
            qc, kc, vc = _mla_qkv(xc.reshape(batch, ctx_len, d), mod_ctx, ng, win, qn, kvn, wqb,
                                  wkvb, _identity_rope_tables(ctx_len), tm=ctx_len)
            a_lat = _mla_attn(ql, kc, vc, kl, vl, tq=MLA_Q_TILE, kv_chunk=MLA_KV_CHUNK)
            xl = _proj_ffn(xl, a_lat.reshape(batch * seq_len, -1), mod_lat, ng, wo, wgu, wd, i,
                           tm=tm)
            if ctx_after:
                a_ctx = _ctx_attn(qc, kc, vc)
                xc = _proj_ffn(xc, a_ctx.reshape(batch * ctx_len, -1), mod_ctx, ng, wo, wgu, wd, i,
                               tm=ctx_len)
        else:
            w_in = na_w_in[j].astype(BF16)
            wo = na_w_o[j].astype(BF16)
            qkv = _norm_proj(xl, mod_lat, ng, w_in, tm=PROJ_TOKEN_TILE)
            kvc = _norm_proj(xc, mod_ctx, ng, w_in[:, NA_WIDTH:], tm=ctx_len)
            a_lat = _na_attn(qkv, kvc, _na_bias_table(na_rpb[j]), batch=batch, seq_len=seq_len,
                             ctx_len=ctx_len)
            xl = _proj_ffn(xl, a_lat, mod_lat, ng, wo, wgu, wd, i, tm=tm)
            if ctx_after:
                raise NotImplementedError("context output of a neighbourhood layer")
    return xl.reshape(batch, seq_len, d)
```
